```python
import jax
import jax.numpy as jnp
from jax import lax
import numpy as np


D_MODEL = 1024
BATCH = 8
SEQ = 4096
DEPTH = 4

GRID_W = 64
CTX_LEN = 256
D_MIX = D_MODEL
D_A = D_MIX // 2
N_HEADS_A = 8
HEAD_A = D_A // N_HEADS_A
CONV_A = 4
LRU_C = 8.0
D_B = D_MIX // 4
POOL_WINDOWS = (2, 4, 8, 16)
N_GROUPS_B = len(POOL_WINDOWS)
GROUP_B = D_B // N_GROUPS_B
D_C = D_MIX // 4
N_GROUPS_C = 4
GROUP_C = D_C // N_GROUPS_C
CHUNK = 128
D_IN = 2 * D_A + D_B + 2 * D_C
D_FF = 2816
FFN_CONV = 3
N_MOD = 6
EPS = 1e-6

kernel_name = 'hybrid_pool_lru_gmlp_diffusion_block'


def rmsnorm(x, g):
    xf = x.astype(jnp.float32)
    y = xf * lax.rsqrt(jnp.mean(xf * xf, axis=-1, keepdims=True) + EPS)
    return (y * g.astype(jnp.float32)).astype(x.dtype)


def modulate(x, g, shift, scale):
    return rmsnorm(x, g) * (1 + scale) + shift


def dwconv_centred(x, w, b):
    L = x.shape[1]
    left = CONV_A // 2
    right = CONV_A - 1 - left
    xp = jnp.pad(x, ((0, 0), (left, right), (0, 0)))
    y = b
    for k in range(CONV_A):
        y = y + xp[:, k:k + L] * w[k]
    return y


def block_diag(xh, w, b):
    y = jnp.einsum('blhi,hij->blhj', xh, w)
    return y.reshape(y.shape[0], y.shape[1], -1) + b


def rglru_coeffs(xf, w_a, b_a, w_x, b_x, lam):
    B, L, _ = xf.shape
    xh = xf.reshape(B, L, N_HEADS_A, HEAD_A)
    r = jax.nn.sigmoid(block_diag(xh, w_a.astype(jnp.float32), b_a.astype(jnp.float32)))
    i = jax.nn.sigmoid(block_diag(xh, w_x.astype(jnp.float32), b_x.astype(jnp.float32)))
    log_a = -LRU_C * r * jax.nn.softplus(-lam.astype(jnp.float32))
    a = jnp.exp(log_a)
    bterm = jnp.sqrt(-jnp.expm1(2.0 * log_a)) * (i * xf)
    return a, bterm


def _combine(left, right):
    a_l, b_l = left
    a_r, b_r = right
    return a_l * a_r, a_r * b_l + b_r


def linear_scan(a, b, h0, reverse):
    idx = -1 if reverse else 0
    b = b.at[:, idx].add(a[:, idx] * h0)
    _, h = lax.associative_scan(_combine, (a, b), reverse=reverse, axis=1)
    return h


def rglru_bidir(xf, w_a, b_a, w_x, b_x, lam, h0_f, h0_b):
    a_f, b_f = rglru_coeffs(xf, w_a[0], b_a[0], w_x[0], b_x[0], lam[0])
    h_f = linear_scan(a_f, b_f, h0_f, False)
    a_b, b_b = rglru_coeffs(xf, w_a[1], b_a[1], w_x[1], b_x[1], lam[1])
    h_b = linear_scan(a_b, b_b, h0_b, True)
    return h_f, h_b


def pool_group(z, w, b, scale):
    B, L, _ = z.shape
    zf = z.astype(jnp.float32)
    cs = jnp.concatenate([jnp.zeros_like(zf[:, :1]), jnp.cumsum(zf, axis=1)], axis=1)
    t = jnp.arange(L)
    parts = []
    for g, win in enumerate(POOL_WINDOWS):
        sl = slice(g * GROUP_B, (g + 1) * GROUP_B)
        lo = jnp.clip(t - win // 2, 0, L)
        hi = jnp.clip(t - win // 2 + win, 0, L)
        csg = cs[..., sl]
        mean = (csg[:, hi] - csg[:, lo]) / (hi - lo).astype(jnp.float32)[None, :, None]
        parts.append(mean - zf[..., sl])
    p = jnp.stack(parts, axis=2)
    y = jnp.einsum('blgi,gij->blgj', p, w.astype(jnp.float32)).reshape(B, L, D_B) + b
    return (y * scale).astype(z.dtype)


def gmlp_group(z, norm_g, w_s, b_s):
    B, L, _ = z.shape
    z = jax.nn.gelu(z)
    u, v = z[..., :D_C], z[..., D_C:]
    vf = v.astype(jnp.float32)
    mu = jnp.mean(vf, axis=-1, keepdims=True)
    var = jnp.mean(jnp.square(vf - mu), axis=-1, keepdims=True)
    vn = ((vf - mu) * lax.rsqrt(var + EPS) * norm_g.astype(jnp.float32)).astype(z.dtype)
    vh = vn.reshape(B, L // CHUNK, CHUNK, N_GROUPS_C, GROUP_C)
    s = jnp.einsum('gij,bnjgc->bnigc', w_s, vh) + b_s.T[None, None, :, :, None]
    return u * s.reshape(B, L, D_C)


def token_mix(z, y_rec, pool_w, pool_b, pool_scale, gmlp_norm, gmlp_w_s, gmlp_b_s, w_out):
    ya = jax.nn.gelu(z[..., D_A:2 * D_A]) * y_rec.astype(z.dtype)
    yb = pool_group(z[..., 2 * D_A:2 * D_A + D_B], pool_w, pool_b, pool_scale)
    yc = gmlp_group(z[..., 2 * D_A + D_B:], gmlp_norm, gmlp_w_s, gmlp_b_s)
    return jnp.concatenate([ya, yb, yc], axis=-1) @ w_out


def conv_ffn(h, w_up, conv_w, conv_b, w_down, rows):
    B, L, _ = h.shape
    z = (h @ w_up).reshape(B, rows, L // rows, 2 * D_FF)
    z = lax.conv_general_dilated(
        z, conv_w[:, :, None, :].astype(z.dtype), window_strides=(1, 1), padding='SAME',
        dimension_numbers=('NHWC', 'HWIO', 'NHWC'), feature_group_count=2 * D_FF) + conv_b
    z = z.reshape(B, L, 2 * D_FF)
    return (jax.nn.gelu(z[..., :D_FF]) * z[..., D_FF:]) @ w_down


def setup_inputs(seed: int = 0) -> dict:
    key = jax.random.key(seed)
    ks = jax.random.split(key, 29)
    f32 = jnp.float32

    def nrm(k, shape, s):
        return jax.random.normal(k, shape, f32) * s

    def gain(k, shape):
        return 1.0 + 0.1 * jax.random.normal(k, shape, f32)

    u = jax.random.uniform(ks[16], (DEPTH, 2, D_A), f32, 0.9, 0.999)
    sig = u ** (1.0 / LRU_C)
    return {
        'x': nrm(ks[0], (BATCH, SEQ, D_MODEL), 1.0),
        'c': nrm(ks[1], (BATCH, D_MODEL), 1.0),
        'ctx': nrm(ks[2], (BATCH, CTX_LEN, D_MODEL), 1.0),
        'c_ctx': nrm(ks[3], (D_MODEL,), 1.0),
        'w_mod': nrm(ks[4], (DEPTH, D_MODEL, N_MOD * D_MODEL), D_MODEL ** -0.5),
        'b_mod': nrm(ks[5], (DEPTH, N_MOD * D_MODEL), 0.02),
        'g_pre_mix': gain(ks[6], (DEPTH, D_MODEL)),
        'g_post_mix': gain(ks[7], (DEPTH, D_MODEL)),
        'g_pre_ffn': gain(ks[8], (DEPTH, D_MODEL)),
        'g_post_ffn': gain(ks[9], (DEPTH, D_MODEL)),
        'w_in': nrm(ks[10], (DEPTH, D_MODEL, D_IN), D_MODEL ** -0.5),
        'conv_a_w': nrm(ks[11], (DEPTH, CONV_A, D_A), CONV_A ** -0.5),
        'conv_a_b': nrm(ks[12], (DEPTH, D_A), 0.02),
        'lru_w_a': nrm(ks[13], (DEPTH, 2, N_HEADS_A, HEAD_A, HEAD_A), HEAD_A ** -0.5),
        'lru_b_a': nrm(ks[14], (DEPTH, 2, D_A), 0.02),
        'lru_w_x': nrm(ks[15], (DEPTH, 2, N_HEADS_A, HEAD_A, HEAD_A), HEAD_A ** -0.5),
        'lru_b_x': nrm(ks[17], (DEPTH, 2, D_A), 0.02),
        'lru_lam': jnp.log(sig) - jnp.log1p(-sig),
        'pool_w': nrm(ks[18], (DEPTH, N_GROUPS_B, GROUP_B, GROUP_B), GROUP_B ** -0.5),
        'pool_b': nrm(ks[19], (DEPTH, D_B), 0.02),
        'pool_scale': gain(ks[20], (DEPTH, D_B)),
        'gmlp_norm': gain(ks[21], (DEPTH, D_C)),
        'gmlp_w_s': nrm(ks[22], (DEPTH, N_GROUPS_C, CHUNK, CHUNK), CHUNK ** -0.5),
        'gmlp_b_s': gain(ks[23], (DEPTH, N_GROUPS_C, CHUNK)),
        'w_out': nrm(ks[24], (DEPTH, D_MIX, D_MODEL), D_MIX ** -0.5),
        'ffn_w_up': nrm(ks[25], (DEPTH, D_MODEL, 2 * D_FF), D_MODEL ** -0.5),
        'ffn_conv_w': nrm(ks[26], (DEPTH, FFN_CONV, FFN_CONV, 2 * D_FF), 1.0 / FFN_CONV),
        'ffn_conv_b': nrm(ks[27], (DEPTH, 2 * D_FF), 0.02),
        'ffn_w_down': nrm(ks[28], (DEPTH, D_FF, D_MODEL), D_FF ** -0.5),
    }


def reference(x, c, ctx, c_ctx, w_mod, b_mod, g_pre_mix, g_post_mix, g_pre_ffn, g_post_ffn,
              w_in, conv_a_w, conv_a_b, lru_w_a, lru_b_a, lru_w_x, lru_b_x, lru_lam,
              pool_w, pool_b, pool_scale, gmlp_norm, gmlp_w_s, gmlp_b_s, w_out,
              ffn_w_up, ffn_conv_w, ffn_conv_b, ffn_w_down):
    B, L, _ = x.shape
    rows = L // GRID_W
    silu_c = jax.nn.silu(c)
    silu_cc = jax.nn.silu(c_ctx)
    zero_state = jnp.zeros((B, D_A), jnp.float32)
    h_lat, h_ctx = x, ctx
    for l in range(DEPTH):
        last = l == DEPTH - 1
        mod_l = jnp.split((silu_c @ w_mod[l] + b_mod[l]).reshape(B, 1, N_MOD * D_MODEL), N_MOD, axis=-1)
        mod_c = jnp.split((silu_cc @ w_mod[l] + b_mod[l]).reshape(1, 1, N_MOD * D_MODEL), N_MOD, axis=-1)

        hl = modulate(h_lat, g_pre_mix[l], mod_l[0], mod_l[1])
        hc = modulate(h_ctx, g_pre_mix[l], mod_c[0], mod_c[1])
        zl = hl @ w_in[l]
        zc = hc @ (w_in[l][:, :D_A] if last else w_in[l])

        xa_c = dwconv_centred(zc[..., :D_A], conv_a_w[l], conv_a_b[l]).astype(jnp.float32)
        hf_c, hb_c = rglru_bidir(xa_c, lru_w_a[l], lru_b_a[l], lru_w_x[l], lru_b_x[l], lru_lam[l],
                                 zero_state, zero_state)
        xa_l = dwconv_centred(zl[..., :D_A], conv_a_w[l], conv_a_b[l]).astype(jnp.float32)
        hf_l, hb_l = rglru_bidir(xa_l, lru_w_a[l], lru_b_a[l], lru_w_x[l], lru_b_x[l], lru_lam[l],
                                 hf_c[:, -1], hb_c[:, 0])

        mix_l = token_mix(zl, hf_l + hb_l, pool_w[l], pool_b[l], pool_scale[l],
                          gmlp_norm[l], gmlp_w_s[l], gmlp_b_s[l], w_out[l])
        h_lat = h_lat + mod_l[2] * rmsnorm(mix_l, g_post_mix[l])
        if not last:
            mix_c = token_mix(zc, hf_c + hb_c, pool_w[l], pool_b[l], pool_scale[l],
                              gmlp_norm[l], gmlp_w_s[l], gmlp_b_s[l], w_out[l])
            h_ctx = h_ctx + mod_c[2] * rmsnorm(mix_c, g_post_mix[l])

        hl = modulate(h_lat, g_pre_ffn[l], mod_l[3], mod_l[4])
        f_l = conv_ffn(hl, ffn_w_up[l], ffn_conv_w[l], ffn_conv_b[l], ffn_w_down[l], rows)
        h_lat = h_lat + mod_l[5] * rmsnorm(f_l, g_post_ffn[l])
        if not last:
            hc = modulate(h_ctx, g_pre_ffn[l], mod_c[3], mod_c[4])
            f_c = conv_ffn(hc, ffn_w_up[l], ffn_conv_w[l], ffn_conv_b[l], ffn_w_down[l], 1)
            h_ctx = h_ctx + mod_c[5] * rmsnorm(f_c, g_post_ffn[l])
    return h_lat
```

```python
import functools
import math

import jax
import jax.numpy as jnp
from jax import lax
from jax.experimental import pallas as pl
from jax.experimental.pallas import tpu as pltpu

GRID_W = 64
CONV_A = 4
LRU_C = 8.0
N_HEADS_A = 8
POOL_HALF = (1, 2, 4, 8)
N_GROUPS_C = 4
CHUNK = 128
N_MOD = 6
EPS = 1e-6
GELU_K = math.sqrt(2.0 / math.pi)
GELU_C = 0.044715

SUBLANES = 8
V7X_VMEM_LIMIT = 56 * 1024 * 1024

FFN_NC = 256

F32 = jnp.float32
BF16 = jnp.bfloat16


def _sigmoid(x):
    return 0.5 * (jnp.tanh(0.5 * x) + 1.0)


def _gelu(x):
    return 0.5 * x * (1.0 + jnp.tanh(GELU_K * (x + GELU_C * (x * x * x))))


def _rms(x, g):
    ms = jnp.mean(x * x, axis=-1, keepdims=True)
    return x * lax.rsqrt(ms + EPS) * g


def _bdot(a, b):
    return jnp.dot(a, b, preferred_element_type=F32)


def _mul_b(x, v8):
    r, c = x.shape
    return (x.reshape(r // SUBLANES, SUBLANES, c) * v8[None]).reshape(r, c)


def _add_b(x, v8):
    r, c = x.shape
    return (x.reshape(r // SUBLANES, SUBLANES, c) + v8[None]).reshape(r, c)


def _modulate(x, g, shift8, scale8):
    return _add_b(_mul_b(_rms(x, g), 1.0 + scale8), shift8)


def _mod_slice(mod_ref, k, d):
    return mod_ref[:, k * d:(k + 1) * d]


def _const_spec(shape):
    nd = len(shape)
    return pl.BlockSpec(shape, lambda *_: (0,) * nd, pipeline_mode=pl.Buffered(1))


def _mod_kernel(c_ref, w_ref, b_ref, o_ref):
    c = c_ref[...]
    s = c * _sigmoid(c)
    o_ref[0] = jnp.dot(s, w_ref[0], preferred_element_type=F32,
                       precision=lax.Precision.HIGHEST) + b_ref[0]


def _mod_call(c_all, w_mod, b_mod):
    depth, d, nm = w_mod.shape
    rows = c_all.shape[0]
    nt = 1536
    return pl.pallas_call(
        _mod_kernel,
        grid=(depth, nm // nt),
        in_specs=[
            pl.BlockSpec((rows, d), lambda l, j: (0, 0)),
            pl.BlockSpec((1, d, nt), lambda l, j: (l, 0, j)),
            pl.BlockSpec((1, 1, nt), lambda l, j: (l, 0, j)),
        ],
        out_specs=pl.BlockSpec((1, rows, nt), lambda l, j: (l, 0, j)),
        out_shape=jax.ShapeDtypeStruct((depth, rows, nm), F32),
        compiler_params=pltpu.CompilerParams(
            dimension_semantics=("arbitrary", "arbitrary"),
            vmem_limit_bytes=40 * 1024 * 1024),
        name="mod_vectors",
    )(c_all, w_mod, b_mod.reshape(depth, 1, nm))


def _in_kernel(h_ref, mod_ref, g_ref, w_ref, kron_ref, gn_ref, bs_ref, z_ref, yc_ref, *, d_keep, d_c):
    d = h_ref.shape[1]
    hm = _modulate(h_ref[...], g_ref[...], _mod_slice(mod_ref, 0, d), _mod_slice(mod_ref, 1, d))
    z = _bdot(hm.astype(BF16), w_ref[...])
    z_ref[...] = z[:, :d_keep]
    zc = _gelu(z[:, d_keep:])
    u = zc[:, :d_c]
    v = zc[:, d_c:]
    mu = jnp.mean(v, axis=-1, keepdims=True)
    vc = v - mu
    var = jnp.mean(vc * vc, axis=-1, keepdims=True)
    vn = (vc * lax.rsqrt(var + EPS) * gn_ref[...]).astype(BF16)
    group = lax.broadcasted_iota(jnp.int32, (1, d_c), 1) // (d_c // N_GROUPS_C)
    s = bs_ref[...]
    for g in range(N_GROUPS_C):
        s = s + jnp.where(group == g, _bdot(kron_ref[g], vn), 0.0)
    yc_ref[...] = u * s


def _in_call(h, mod8, g_pre, w_in, kron, gnorm, bias_s, *, d_keep, d_c):
    n, d = h.shape
    tm = CHUNK * SUBLANES
    d_in = w_in.shape[1]
    return pl.pallas_call(
        functools.partial(_in_kernel, d_keep=d_keep, d_c=d_c),
        grid=(n // tm,),
        in_specs=[
            pl.BlockSpec((tm, d), lambda i: (i, 0)),
            _const_spec(mod8.shape),
            _const_spec(g_pre.shape),
            _const_spec(w_in.shape),
            _const_spec(kron.shape),
            _const_spec(gnorm.shape),
            _const_spec(bias_s.shape),
        ],
        out_specs=[
            pl.BlockSpec((tm, d_keep), lambda i: (i, 0)),
            pl.BlockSpec((tm, d_c), lambda i: (i, 0)),
        ],
        out_shape=[
            jax.ShapeDtypeStruct((n, d_keep), F32),
            jax.ShapeDtypeStruct((n, d_c), F32),
        ],
        compiler_params=pltpu.CompilerParams(
            dimension_semantics=("arbitrary",), vmem_limit_bytes=V7X_VMEM_LIMIT),
        name="mix_in_proj",
    )(h, mod8, g_pre, w_in, kron, gnorm, bias_s)


def _rglru_tile(zm_ref, zp_ref, zn_ref, has_prev, has_next, cw_ref, cb_ref, wax_ref, bax_ref, lam_ref,
                a_s, b_s, carry, hout_ref, *, reverse):
    tm, da = zm_ref.shape
    ext = jnp.concatenate([zp_ref[...] * has_prev, zm_ref[...], zn_ref[...] * has_next], axis=0)
    cw = cw_ref[...]
    xa = cb_ref[...]
    for k in range(CONV_A):
        xa = xa + ext[k * SUBLANES:k * SUBLANES + tm] * cw[k:k + 1]
    pre = _bdot(xa.astype(BF16), wax_ref[...]) + bax_ref[...]
    r = _sigmoid(pre[:, :da])
    gate_i = _sigmoid(pre[:, da:])
    lam = lam_ref[...]
    softplus_neg_lam = jnp.maximum(-lam, 0.0) + jnp.log1p(jnp.exp(-jnp.abs(lam)))
    log_a = (-LRU_C * r) * softplus_neg_lam
    a = jnp.exp(log_a)
    one_minus_a2 = -jnp.tanh(log_a) * (a * a + 1.0)
    a_s[...] = a
    b_s[...] = jnp.sqrt(one_minus_a2) * (gate_i * xa)
    h = carry[...]
    steps = tm // SUBLANES
    order = range(steps - 1, -1, -1) if reverse else range(steps)
    for t in order:
        sl = pl.ds(t * SUBLANES, SUBLANES)
        h = a_s[sl, :] * h + b_s[sl, :]
        hout_ref[sl, :] = h
    carry[...] = h


def _scan_kernel(zm_ref, zp_ref, zn_ref, cw_ref, cb_ref, wax_ref, bax_ref, lam_ref, h0_ref,
                 hout_ref, state_ref, a_s, b_s, carry, *, reverse):
    i = pl.program_id(0)
    nt = pl.num_programs(0)
    tile = nt - 1 - i if reverse else i

    @pl.when(i == 0)
    def _():
        carry[...] = h0_ref[...]

    has_prev = (tile > 0).astype(F32)
    has_next = (tile < nt - 1).astype(F32)
    _rglru_tile(zm_ref, zp_ref, zn_ref, has_prev, has_next, cw_ref, cb_ref, wax_ref, bax_ref, lam_ref,
                a_s, b_s, carry, hout_ref, reverse=reverse)
    state_ref[...] = carry[...]


def _halo_specs(tm, width, col_block, n_rows, before, after, tile_of):
    pb = tm // before
    nb = tm // after
    last_nb = n_rows // after - 1
    prev = pl.BlockSpec((before, width), lambda i: (jnp.maximum(tile_of(i) * pb - 1, 0), col_block))
    nxt = pl.BlockSpec((after, width), lambda i: (jnp.minimum((tile_of(i) + 1) * nb, last_nb), col_block))
    return prev, nxt


def _scan_call(z, cw, cb, wax, bax, lam, h0, *, reverse, tm):
    n = z.shape[0]
    da = cw.shape[1]
    nt = n // tm
    tile_of = (lambda i: nt - 1 - i) if reverse else (lambda i: i)
    prev, nxt = _halo_specs(tm, da, 0, n, (CONV_A // 2) * SUBLANES, SUBLANES, tile_of)
    return pl.pallas_call(
        functools.partial(_scan_kernel, reverse=reverse),
        grid=(nt,),
        in_specs=[
            pl.BlockSpec((tm, da), lambda i: (tile_of(i), 0)), prev, nxt,
            _const_spec(cw.shape), _const_spec(cb.shape), _const_spec(wax.shape),
            _const_spec(bax.shape), _const_spec(lam.shape), _const_spec(h0.shape),
        ],
        out_specs=[
            pl.BlockSpec((tm, da), lambda i: (tile_of(i), 0)),
            pl.BlockSpec((SUBLANES, da), lambda i: (0, 0)),
        ],
        out_shape=[
            jax.ShapeDtypeStruct((n, da), F32),
            jax.ShapeDtypeStruct((SUBLANES, da), F32),
        ],
        scratch_shapes=[
            pltpu.VMEM((tm, da), F32), pltpu.VMEM((tm, da), F32), pltpu.VMEM((SUBLANES, da), F32),
        ],
        compiler_params=pltpu.CompilerParams(
            dimension_semantics=("arbitrary",), vmem_limit_bytes=32 * 1024 * 1024),
        name="rglru_scan_bwd" if reverse else "rglru_scan_fwd",
    )(z, z, z, cw, cb, wax, bax, lam, h0)


def _pool_means(ext, zmain_rows, t0, seq_len):
    halo = POOL_HALF[-1]
    e_rows, width = ext.shape
    e = e_rows // SUBLANES
    tm = zmain_rows

    def rows(arr, start, count):
        return arr[start * SUBLANES:(start + count) * SUBLANES]

    s2 = rows(ext, 0, e - 1) + rows(ext, 1, e - 1)
    s4 = rows(s2, 0, e - 3) + rows(s2, 2, e - 3)
    s8 = rows(s4, 0, e - 7) + rows(s4, 4, e - 7)
    s16 = rows(s8, 0, e - 15) + rows(s8, 8, e - 15)
    t = tm // SUBLANES
    sums = (rows(s2, halo - 1, t), rows(s4, halo - 2, t), rows(s8, halo - 4, t), rows(s16, 0, t))
    group = lax.broadcasted_iota(jnp.int32, (1, width), 1) // (width // len(POOL_HALF))
    win = sums[-1]
    half = jnp.full((1, width), POOL_HALF[-1], jnp.int32)
    for g in range(len(POOL_HALF) - 2, -1, -1):
        win = jnp.where(group == g, sums[g], win)
        half = jnp.where(group == g, POOL_HALF[g], half)
    pos = t0 + lax.broadcasted_iota(jnp.int32, (tm, width), 0) // SUBLANES
    lo = jnp.maximum(pos - half, 0)
    hi = jnp.minimum(pos + half, seq_len)
    return win / (hi - lo).astype(F32)


def _mix_kernel(za_ref, zap_ref, zan_ref, zg_ref, zb_ref, zbp_ref, zbn_ref, yc_ref, hb_ref, h_ref,
                mod_ref, gpost_ref, cw_ref, cb_ref, wax_ref, bax_ref, lam_ref, h0_ref,
                pw_ref, pb_ref, ps_ref, wout_ref,
                out_ref, state_ref, a_s, b_s, carry, hf_s, *, seq_len):
    i = pl.program_id(0)
    nt = pl.num_programs(0)
    tm, d = h_ref.shape

    @pl.when(i == 0)
    def _():
        carry[...] = h0_ref[...]

    has_prev = (i > 0).astype(F32)
    has_next = (i < nt - 1).astype(F32)
    _rglru_tile(za_ref, zap_ref, zan_ref, has_prev, has_next, cw_ref, cb_ref, wax_ref, bax_ref, lam_ref,
                a_s, b_s, carry, hf_s, reverse=False)
    state_ref[...] = carry[...]

    ya = _gelu(zg_ref[...]) * (hf_s[...] + hb_ref[...])

    zb = zb_ref[...]
    ext = jnp.concatenate([zbp_ref[...] * has_prev, zb, zbn_ref[...] * has_next], axis=0)
    t0 = i * (tm // SUBLANES)
    p = _pool_means(ext, tm, t0, seq_len) - zb
    yb = (_bdot(p.astype(BF16), pw_ref[...]) + pb_ref[...]) * ps_ref[...]

    cat = jnp.concatenate([ya, yb, yc_ref[...]], axis=1).astype(BF16)
    mix = _bdot(cat, wout_ref[...])
    out_ref[...] = h_ref[...] + _mul_b(_rms(mix, gpost_ref[...]), _mod_slice(mod_ref, 2, d))


def _mix_call(z, yc, hb, h, mod8, g_post, cw, cb, wax, bax, lam, h0, pool_w, pool_b, pool_s, w_out,
              *, seq_len, tm):
    n, d = h.shape
    da = cw.shape[1]
    db = pool_w.shape[0]
    nt = n // tm
    ident = lambda i: i
    za_prev, za_next = _halo_specs(tm, da, 0, n, (CONV_A // 2) * SUBLANES, SUBLANES, ident)
    pool_halo = POOL_HALF[-1] * SUBLANES
    zb_col = (2 * da) // db
    zb_prev, zb_next = _halo_specs(tm, db, zb_col, n, pool_halo, pool_halo, ident)
    consts = (mod8, g_post, cw, cb, wax, bax, lam, h0, pool_w, pool_b, pool_s, w_out)
    return pl.pallas_call(
        functools.partial(_mix_kernel, seq_len=seq_len),
        grid=(nt,),
        in_specs=[
            pl.BlockSpec((tm, da), lambda i: (i, 0)), za_prev, za_next,
            pl.BlockSpec((tm, da), lambda i: (i, 1)),
            pl.BlockSpec((tm, db), lambda i: (i, zb_col)), zb_prev, zb_next,
            pl.BlockSpec((tm, yc.shape[1]), lambda i: (i, 0)),
            pl.BlockSpec((tm, da), lambda i: (i, 0)),
            pl.BlockSpec((tm, d), lambda i: (i, 0)),
        ] + [_const_spec(a.shape) for a in consts],
        out_specs=[
            pl.BlockSpec((tm, d), lambda i: (i, 0)),
            pl.BlockSpec((SUBLANES, da), lambda i: (0, 0)),
        ],
        out_shape=[
            jax.ShapeDtypeStruct((n, d), F32),
            jax.ShapeDtypeStruct((SUBLANES, da), F32),
        ],
        scratch_shapes=[
            pltpu.VMEM((tm, da), F32), pltpu.VMEM((tm, da), F32), pltpu.VMEM((SUBLANES, da), F32),
            pltpu.VMEM((tm, da), F32),
        ],
        compiler_params=pltpu.CompilerParams(
            dimension_semantics=("arbitrary",), vmem_limit_bytes=V7X_VMEM_LIMIT),
        name="token_mix",
    )(z, z, z, z, z, z, z, yc, hb, h, *consts)


def _taps(z, zl, zr, cw, row):
    return zl * cw[3 * row:3 * row + 1] + z * cw[3 * row + 1:3 * row + 2] + zr * cw[3 * row + 2:3 * row + 3]


def _ffn_grid_kernel(hrow_ref, mod_ref, gpre_ref, gpost_ref, wup_ref, cw_ref, cb_ref, wdown_ref,
                     out_ref, ring, hm_s, acc_s, hres_s, *, n_img_rows):
    s = pl.program_id(0)
    rm, d = hrow_ref.shape
    nch = wdown_ref.shape[0]
    nc = wdown_ref.shape[1]

    @pl.when(s == 0)
    def _():
        for k in range(2 * nch):
            ring[0, k] = jnp.broadcast_to(cb_ref[k], (rm, nc))
            ring[1, k] = jnp.zeros((rm, nc), F32)
        hres_s[...] = hrow_ref[...]

    hm_s[...] = _modulate(hrow_ref[...], gpre_ref[...], _mod_slice(mod_ref, 3, d),
                          _mod_slice(mod_ref, 4, d)).astype(BF16)
    acc_s[...] = jnp.zeros_like(acc_s)
    slot = s % 2
    oslot = 1 - slot
    has_row = (s < n_img_rows).astype(F32)
    zero_pos = jnp.zeros((SUBLANES, nc), F32)

    def chunk(c, carry_unused):
        done = []
        for half in range(2):
            k = half * nch + c
            z = _bdot(hm_s[...], wup_ref[k])
            zl = jnp.concatenate([zero_pos, z[:-SUBLANES]], axis=0)
            zr = jnp.concatenate([z[SUBLANES:], zero_pos], axis=0)
            cw = cw_ref[k]
            done.append(ring[oslot, k] + _taps(z, zl, zr, cw * has_row, 2))
            ring[oslot, k] = cb_ref[k] + _taps(z, zl, zr, cw, 0)
            ring[slot, k] = ring[slot, k] + _taps(z, zl, zr, cw, 1)
        y = (_gelu(done[0]) * done[1]).astype(BF16)
        acc_s[...] += _bdot(y, wdown_ref[c])
        return carry_unused

    lax.fori_loop(0, nch, chunk, 0)
    out_ref[...] = hres_s[...] + _mul_b(_rms(acc_s[...], gpost_ref[...]), _mod_slice(mod_ref, 5, d))
    hres_s[...] = hrow_ref[...]


def _ffn_grid_call(h, mod8, g_pre, g_post, w_up, cw, cb, w_down, *, grid_w):
    n, d = h.shape
    rm = grid_w * SUBLANES
    n_img_rows = n // rm
    nch, nc, _ = w_down.shape
    consts = (mod8, g_pre, g_post, w_up, cw, cb, w_down)
    return pl.pallas_call(
        functools.partial(_ffn_grid_kernel, n_img_rows=n_img_rows),
        grid=(n_img_rows + 1,),
        in_specs=[
            pl.BlockSpec((rm, d), lambda s: (jnp.minimum(s, n_img_rows - 1), 0)),
        ] + [_const_spec(a.shape) for a in consts],
        out_specs=pl.BlockSpec((rm, d), lambda s: (jnp.maximum(s - 1, 0), 0)),
        out_shape=jax.ShapeDtypeStruct((n, d), F32),
        scratch_shapes=[
            pltpu.VMEM((2, 2 * nch, rm, nc), F32),
            pltpu.VMEM((rm, d), BF16),
            pltpu.VMEM((rm, d), F32),
            pltpu.VMEM((rm, d), F32),
        ],
        compiler_params=pltpu.CompilerParams(
            dimension_semantics=("arbitrary",), vmem_limit_bytes=V7X_VMEM_LIMIT),
        name="conv_ffn_grid",
    )(h, *consts)


def _ffn_seq_kernel(h_ref, hp_ref, hn_ref, mod_ref, gpre_ref, gpost_ref, wup_ref, cw_ref, cb_ref, wdown_ref,
                    out_ref, hm_s, acc_s):
    i = pl.program_id(0)
    nt = pl.num_programs(0)
    tm, d = h_ref.shape
    nch = wdown_ref.shape[0]
    shift8 = _mod_slice(mod_ref, 3, d)
    scale8 = _mod_slice(mod_ref, 4, d)
    gpre = gpre_ref[...]
    has_prev = (i > 0).astype(F32)
    has_next = (i < nt - 1).astype(F32)
    hm_s[0:SUBLANES] = (_modulate(hp_ref[...], gpre, shift8, scale8) * has_prev).astype(BF16)
    hm_s[SUBLANES:SUBLANES + tm] = _modulate(h_ref[...], gpre, shift8, scale8).astype(BF16)
    hm_s[SUBLANES + tm:2 * SUBLANES + tm] = (_modulate(hn_ref[...], gpre, shift8, scale8) * has_next).astype(BF16)
    acc_s[...] = jnp.zeros_like(acc_s)

    def chunk(c, carry_unused):
        done = []
        for half in range(2):
            k = half * nch + c
            z = _bdot(hm_s[...], wup_ref[k])
            done.append(cb_ref[k] + _taps(z[SUBLANES:SUBLANES + tm], z[0:tm], z[2 * SUBLANES:2 * SUBLANES + tm],
                                          cw_ref[k], 1))
        y = (_gelu(done[0]) * done[1]).astype(BF16)
        acc_s[...] += _bdot(y, wdown_ref[c])
        return carry_unused

    lax.fori_loop(0, nch, chunk, 0)
    out_ref[...] = h_ref[...] + _mul_b(_rms(acc_s[...], gpost_ref[...]), _mod_slice(mod_ref, 5, d))


def _ffn_seq_call(h, mod8, g_pre, g_post, w_up, cw, cb, w_down, *, tm):
    n, d = h.shape
    nt = n // tm
    prev, nxt = _halo_specs(tm, d, 0, n, SUBLANES, SUBLANES, lambda i: i)
    consts = (mod8, g_pre, g_post, w_up, cw, cb, w_down)
    return pl.pallas_call(
        _ffn_seq_kernel,
        grid=(nt,),
        in_specs=[pl.BlockSpec((tm, d), lambda i: (i, 0)), prev, nxt] + [_const_spec(a.shape) for a in consts],
        out_specs=pl.BlockSpec((tm, d), lambda i: (i, 0)),
        out_shape=jax.ShapeDtypeStruct((n, d), F32),
        scratch_shapes=[
            pltpu.VMEM((tm + 2 * SUBLANES, d), BF16),
            pltpu.VMEM((tm, d), F32),
        ],
        compiler_params=pltpu.CompilerParams(
            dimension_semantics=("arbitrary",), vmem_limit_bytes=V7X_VMEM_LIMIT),
        name="conv_ffn_seq",
    )(h, h, h, *consts)


def _to_rows(x):
    b, l, d = x.shape
    return jnp.transpose(x, (1, 0, 2)).reshape(l * b, d)


def _from_rows(h, b):
    n, d = h.shape
    return jnp.transpose(h.reshape(n // b, b, d), (1, 0, 2))


def _block_diag(w):
    hh, ii, jj = w.shape
    eye = jnp.eye(hh, dtype=w.dtype)
    return (eye[:, None, :, None] * w[:, :, None, :]).reshape(hh * ii, hh * jj)


def kernel(x, c, ctx, c_ctx, w_mod, b_mod, g_pre_mix, g_post_mix, g_pre_ffn, g_post_ffn, w_in, conv_a_w, conv_a_b, lru_w_a, lru_b_a, lru_w_x, lru_b_x, lru_lam, pool_w, pool_b, pool_scale, gmlp_norm, gmlp_w_s, gmlp_b_s, w_out, ffn_w_up, ffn_conv_w, ffn_conv_b, ffn_w_down):
    batch, seq_len, d = x.shape
    ctx_len = ctx.shape[1]
    depth = w_mod.shape[0]
    assert batch == SUBLANES, "the (t, b) row layout puts the batch on the 8 sublanes"
    d_a = conv_a_w.shape[2]
    d_b = pool_b.shape[1]
    d_c = gmlp_norm.shape[1]
    d_ff = ffn_w_down.shape[1]
    assert gmlp_w_s.shape[2] == CHUNK and seq_len % CHUNK == 0 and ctx_len % CHUNK == 0
    assert seq_len % GRID_W == 0 and d_ff % FFN_NC == 0
    nch = d_ff // FFN_NC
    d_keep = 2 * d_a + d_b

    h_lat = _to_rows(x)
    h_ctx = _to_rows(ctx)

    pad = jnp.zeros((2 * SUBLANES - batch - 1, d), F32)
    c_all = jnp.concatenate([c, c_ctx[None], pad], axis=0)
    mod = _mod_call(c_all, w_mod, b_mod)
    zero_state = jnp.zeros((SUBLANES, d_a), F32)
    eye_b = jnp.eye(SUBLANES, dtype=F32)

    for l in range(depth):
        last = l == depth - 1
        mod_lat = mod[l, :batch]
        mod_ctx = jnp.broadcast_to(mod[l, batch:batch + 1], (SUBLANES, N_MOD * d))
        row = lambda v: v.reshape(1, -1)

        w_in_l = w_in[l].astype(BF16)
        kron = jnp.stack([jnp.kron(gmlp_w_s[l, g], eye_b) for g in range(N_GROUPS_C)]).astype(BF16)
        bias_s = jnp.repeat(jnp.repeat(gmlp_b_s[l].T, SUBLANES, axis=0), d_c // N_GROUPS_C, axis=1)
        wax = [jnp.concatenate([_block_diag(lru_w_a[l, k]), _block_diag(lru_w_x[l, k])], axis=1).astype(BF16)
               for k in range(2)]
        bax = [jnp.concatenate([lru_b_a[l, k], lru_b_x[l, k]]).reshape(1, -1) for k in range(2)]
        lam = [row(lru_lam[l, k]) for k in range(2)]
        cw_a, cb_a = conv_a_w[l], row(conv_a_b[l])
        pool_w_l = _block_diag(pool_w[l]).astype(BF16)
        w_out_l = w_out[l].astype(BF16)
        w_up_l = jnp.transpose(ffn_w_up[l].reshape(d, 2 * nch, FFN_NC), (1, 0, 2)).astype(BF16)
        cw_f = jnp.transpose(ffn_conv_w[l].reshape(9, 2 * nch, FFN_NC), (1, 0, 2))
        cb_f = ffn_conv_b[l].reshape(2 * nch, 1, FFN_NC)
        w_down_l = ffn_w_down[l].reshape(nch, FFN_NC, d).astype(BF16)

        def mix(h, mod8, h0_f, h0_b, length):
            z, yc = _in_call(h, mod8, row(g_pre_mix[l]), w_in_l, kron, row(gmlp_norm[l]), bias_s,
                             d_keep=d_keep, d_c=d_c)
            hb, state_b = _scan_call(z, cw_a, cb_a, wax[1], bax[1], lam[1], h0_b, reverse=True, tm=512)
            h_new, state_f = _mix_call(z, yc, hb, h, mod8, row(g_post_mix[l]), cw_a, cb_a, wax[0], bax[0],
                                       lam[0], h0_f, pool_w_l, row(pool_b[l]), row(pool_scale[l]), w_out_l,
                                       seq_len=length, tm=512)
            return h_new, state_f, state_b

        h_ctx_mixed, ctx_f, ctx_b = mix(h_ctx, mod_ctx, zero_state, zero_state, ctx_len)
        h_lat, _, _ = mix(h_lat, mod_lat, ctx_f, ctx_b, seq_len)

        ffn_w = (row(g_pre_ffn[l]), row(g_post_ffn[l]), w_up_l, cw_f, cb_f, w_down_l)
        h_lat = _ffn_grid_call(h_lat, mod_lat, *ffn_w, grid_w=GRID_W)
        if not last:
            h_ctx = _ffn_seq_call(h_ctx_mixed, mod_ctx, *ffn_w, tm=512)

    return _from_rows(h_lat, batch)
```

```python
import functools
import math

import jax
import jax.numpy as jnp
from jax import lax
from jax.experimental import pallas as pl
from jax.experimental.pallas import tpu as pltpu

GRID_W = 64
CONV_A = 4
LRU_C = 8.0
N_HEADS_A = 8
POOL_HALF = (1, 2, 4, 8)
N_GROUPS_C = 4
CHUNK = 128
N_MOD = 6
EPS = 1e-6
GELU_K = math.sqrt(2.0 / math.pi)
GELU_C = 0.044715

SUBLANES = 8
LANES = 128
V7X_VMEM_LIMIT = 56 * 1024 * 1024

V7X_VMEM_LIMIT_MAX = 60 * 1024 * 1024

FFN_NC = 256
FFN_ROW_BLOCK = 32

F32 = jnp.float32
BF16 = jnp.bfloat16


def _sigmoid(x):
    return 0.5 * (jnp.tanh(0.5 * x) + 1.0)


def _gelu(x):
    th = jnp.tanh(x * (GELU_K + (GELU_K * GELU_C) * (x * x)))
    hx = 0.5 * x
    return hx + hx * th


def _rms(x, g):
    ms = jnp.mean(x * x, axis=-1, keepdims=True)
    return x * lax.rsqrt(ms + EPS) * g


def _bdot(a, b):
    return jnp.dot(a, b, preferred_element_type=F32)


def _mul_b(x, v8):
    r, c = x.shape
    return (x.reshape(r // SUBLANES, SUBLANES, c) * v8[None]).reshape(r, c)


def _add_b(x, v8):
    r, c = x.shape
    return (x.reshape(r // SUBLANES, SUBLANES, c) + v8[None]).reshape(r, c)


def _modulate(x, g, shift8, scale8):
    return _add_b(_mul_b(_rms(x, g), 1.0 + scale8), shift8)


def _mod_slice(mod_ref, k, d):
    return mod_ref[:, k * d:(k + 1) * d]


def _const_spec(shape):
    nd = len(shape)
    return pl.BlockSpec(shape, lambda *_: (0,) * nd, pipeline_mode=pl.Buffered(1))


def _mod_kernel(c_ref, w_ref, b_ref, o_ref):
    c = c_ref[...]
    s = c * _sigmoid(c)
    o_ref[0] = jnp.dot(s, w_ref[0], preferred_element_type=F32,
                       precision=lax.Precision.HIGHEST) + b_ref[0]


def _mod_call(c_all, w_mod, b_mod):
    depth, d, nm = w_mod.shape
    rows = c_all.shape[0]
    nt = 1536
    return pl.pallas_call(
        _mod_kernel,
        grid=(depth, nm // nt),
        in_specs=[
            pl.BlockSpec((rows, d), lambda l, j: (0, 0)),
            pl.BlockSpec((1, d, nt), lambda l, j: (l, 0, j)),
            pl.BlockSpec((1, 1, nt), lambda l, j: (l, 0, j)),
        ],
        out_specs=pl.BlockSpec((1, rows, nt), lambda l, j: (l, 0, j)),
        out_shape=jax.ShapeDtypeStruct((depth, rows, nm), F32),
        compiler_params=pltpu.CompilerParams(
            dimension_semantics=("arbitrary", "arbitrary"),
            vmem_limit_bytes=40 * 1024 * 1024),
        name="mod_vectors",
    )(c_all, w_mod, b_mod.reshape(depth, 1, nm))


def _in_kernel(h_ref, mod_ref, g_ref, w_ref, kron_ref, gn_ref, bs_ref, z_ref, yc_ref, *, d_keep, d_c):
    d = h_ref.shape[1]
    hm = _modulate(h_ref[...], g_ref[...], _mod_slice(mod_ref, 0, d), _mod_slice(mod_ref, 1, d))
    z = _bdot(hm.astype(BF16), w_ref[...])
    z_ref[...] = z[:, :d_keep]
    zc = _gelu(z[:, d_keep:])
    u = zc[:, :d_c]
    v = zc[:, d_c:]
    mu = jnp.mean(v, axis=-1, keepdims=True)
    vc = v - mu
    var = jnp.mean(vc * vc, axis=-1, keepdims=True)
    vn = (vc * lax.rsqrt(var + EPS) * gn_ref[...]).astype(BF16)
    group = lax.broadcasted_iota(jnp.int32, (1, d_c), 1) // (d_c // N_GROUPS_C)
    s = bs_ref[...]
    for g in range(N_GROUPS_C):
        s = s + jnp.where(group == g, _bdot(kron_ref[g], vn), 0.0)
    yc_ref[...] = u * s


def _in_call(h, mod8, g_pre, w_in, kron, gnorm, bias_s, *, d_keep, d_c):
    n, d = h.shape
    tm = CHUNK * SUBLANES
    d_in = w_in.shape[1]
    return pl.pallas_call(
        functools.partial(_in_kernel, d_keep=d_keep, d_c=d_c),
        grid=(n // tm,),
        in_specs=[
            pl.BlockSpec((tm, d), lambda i: (i, 0)),
            _const_spec(mod8.shape),
            _const_spec(g_pre.shape),
            _const_spec(w_in.shape),
            _const_spec(kron.shape),
            _const_spec(gnorm.shape),
            _const_spec(bias_s.shape),
        ],
        out_specs=[
            pl.BlockSpec((tm, d_keep), lambda i: (i, 0)),
            pl.BlockSpec((tm, d_c), lambda i: (i, 0)),
        ],
        out_shape=[
            jax.ShapeDtypeStruct((n, d_keep), F32),
            jax.ShapeDtypeStruct((n, d_c), F32),
        ],
        compiler_params=pltpu.CompilerParams(
            dimension_semantics=("arbitrary",), vmem_limit_bytes=V7X_VMEM_LIMIT),
        name="mix_in_proj",
    )(h, mod8, g_pre, w_in, kron, gnorm, bias_s)


def _rglru_tile(zm_ref, zp_ref, zn_ref, has_prev, has_next, cw_ref, cb_ref, wax_ref, bax_ref, lam_ref,
                a_s, b_s, carry, hout_ref, *, reverse):
    tm, da = zm_ref.shape
    ext = jnp.concatenate([zp_ref[...] * has_prev, zm_ref[...], zn_ref[...] * has_next], axis=0)
    cw = cw_ref[...]
    xa = cb_ref[...]
    for k in range(CONV_A):
        xa = xa + ext[k * SUBLANES:k * SUBLANES + tm] * cw[k:k + 1]
    pre = _bdot(xa.astype(BF16), wax_ref[...]) + bax_ref[...]
    r = _sigmoid(pre[:, :da])
    gate_i = _sigmoid(pre[:, da:])
    lam = lam_ref[...]
    softplus_neg_lam = jnp.maximum(-lam, 0.0) + jnp.log1p(jnp.exp(-jnp.abs(lam)))
    log_a = (-LRU_C * r) * softplus_neg_lam
    a = jnp.exp(log_a)
    one_minus_a2 = -jnp.tanh(log_a) * (a * a + 1.0)
    a_s[...] = a
    b_s[...] = jnp.sqrt(one_minus_a2) * (gate_i * xa)
    h = carry[...]
    steps = tm // SUBLANES
    order = range(steps - 1, -1, -1) if reverse else range(steps)
    for t in order:
        sl = pl.ds(t * SUBLANES, SUBLANES)
        h = a_s[sl, :] * h + b_s[sl, :]
        hout_ref[sl, :] = h
    carry[...] = h


def _scan_kernel(zm_ref, zp_ref, zn_ref, cw_ref, cb_ref, wax_ref, bax_ref, lam_ref, h0_ref,
                 hout_ref, state_ref, a_s, b_s, carry, *, reverse):
    i = pl.program_id(0)
    nt = pl.num_programs(0)
    tile = nt - 1 - i if reverse else i

    @pl.when(i == 0)
    def _():
        carry[...] = h0_ref[...]

    has_prev = (tile > 0).astype(F32)
    has_next = (tile < nt - 1).astype(F32)
    _rglru_tile(zm_ref, zp_ref, zn_ref, has_prev, has_next, cw_ref, cb_ref, wax_ref, bax_ref, lam_ref,
                a_s, b_s, carry, hout_ref, reverse=reverse)
    state_ref[...] = carry[...]


def _halo_specs(tm, width, col_block, n_rows, before, after, tile_of):
    pb = tm // before
    nb = tm // after
    last_nb = n_rows // after - 1
    prev = pl.BlockSpec((before, width), lambda i: (jnp.maximum(tile_of(i) * pb - 1, 0), col_block))
    nxt = pl.BlockSpec((after, width), lambda i: (jnp.minimum((tile_of(i) + 1) * nb, last_nb), col_block))
    return prev, nxt


def _scan_call(z, cw, cb, wax, bax, lam, h0, *, reverse, tm):
    n = z.shape[0]
    da = cw.shape[1]
    nt = n // tm
    tile_of = (lambda i: nt - 1 - i) if reverse else (lambda i: i)
    prev, nxt = _halo_specs(tm, da, 0, n, (CONV_A // 2) * SUBLANES, SUBLANES, tile_of)
    return pl.pallas_call(
        functools.partial(_scan_kernel, reverse=reverse),
        grid=(nt,),
        in_specs=[
            pl.BlockSpec((tm, da), lambda i: (tile_of(i), 0)), prev, nxt,
            _const_spec(cw.shape), _const_spec(cb.shape), _const_spec(wax.shape),
            _const_spec(bax.shape), _const_spec(lam.shape), _const_spec(h0.shape),
        ],
        out_specs=[
            pl.BlockSpec((tm, da), lambda i: (tile_of(i), 0)),
            pl.BlockSpec((SUBLANES, da), lambda i: (0, 0)),
        ],
        out_shape=[
            jax.ShapeDtypeStruct((n, da), F32),
            jax.ShapeDtypeStruct((SUBLANES, da), F32),
        ],
        scratch_shapes=[
            pltpu.VMEM((tm, da), F32), pltpu.VMEM((tm, da), F32), pltpu.VMEM((SUBLANES, da), F32),
        ],
        compiler_params=pltpu.CompilerParams(
            dimension_semantics=("arbitrary",), vmem_limit_bytes=32 * 1024 * 1024),
        name="rglru_scan_bwd" if reverse else "rglru_scan_fwd",
    )(z, z, z, cw, cb, wax, bax, lam, h0)


def _pool_means(ext, zmain_rows, t0, seq_len):
    halo = POOL_HALF[-1]
    e_rows, width = ext.shape
    e = e_rows // SUBLANES
    tm = zmain_rows

    def rows(arr, start, count):
        return arr[start * SUBLANES:(start + count) * SUBLANES]

    s2 = rows(ext, 0, e - 1) + rows(ext, 1, e - 1)
    s4 = rows(s2, 0, e - 3) + rows(s2, 2, e - 3)
    s8 = rows(s4, 0, e - 7) + rows(s4, 4, e - 7)
    s16 = rows(s8, 0, e - 15) + rows(s8, 8, e - 15)
    t = tm // SUBLANES
    sums = (rows(s2, halo - 1, t), rows(s4, halo - 2, t), rows(s8, halo - 4, t), rows(s16, 0, t))
    group = lax.broadcasted_iota(jnp.int32, (1, width), 1) // (width // len(POOL_HALF))
    win = sums[-1]
    half = jnp.full((1, width), POOL_HALF[-1], jnp.int32)
    for g in range(len(POOL_HALF) - 2, -1, -1):
        win = jnp.where(group == g, sums[g], win)
        half = jnp.where(group == g, POOL_HALF[g], half)
    pos = t0 + lax.broadcasted_iota(jnp.int32, (tm, width), 0) // SUBLANES
    lo = jnp.maximum(pos - half, 0)
    hi = jnp.minimum(pos + half, seq_len)
    return win / (hi - lo).astype(F32)


def _mix_kernel(za_ref, zap_ref, zan_ref, zg_ref, zb_ref, zbp_ref, zbn_ref, yc_ref, hb_ref, h_ref,
                mod_ref, gpost_ref, cw_ref, cb_ref, wax_ref, bax_ref, lam_ref, h0_ref,
                pw_ref, pb_ref, ps_ref, wout_ref,
                out_ref, state_ref, a_s, b_s, carry, hf_s, *, seq_len):
    i = pl.program_id(0)
    nt = pl.num_programs(0)
    tm, d = h_ref.shape

    @pl.when(i == 0)
    def _():
        carry[...] = h0_ref[...]

    has_prev = (i > 0).astype(F32)
    has_next = (i < nt - 1).astype(F32)
    _rglru_tile(za_ref, zap_ref, zan_ref, has_prev, has_next, cw_ref, cb_ref, wax_ref, bax_ref, lam_ref,
                a_s, b_s, carry, hf_s, reverse=False)
    state_ref[...] = carry[...]

    ya = _gelu(zg_ref[...]) * (hf_s[...] + hb_ref[...])

    zb = zb_ref[...]
    ext = jnp.concatenate([zbp_ref[...] * has_prev, zb, zbn_ref[...] * has_next], axis=0)
    t0 = i * (tm // SUBLANES)
    p = _pool_means(ext, tm, t0, seq_len) - zb
    yb = (_bdot(p.astype(BF16), pw_ref[...]) + pb_ref[...]) * ps_ref[...]

    cat = jnp.concatenate([ya, yb, yc_ref[...]], axis=1).astype(BF16)
    mix = _bdot(cat, wout_ref[...])
    out_ref[...] = h_ref[...] + _mul_b(_rms(mix, gpost_ref[...]), _mod_slice(mod_ref, 2, d))


def _mix_call(z, yc, hb, h, mod8, g_post, cw, cb, wax, bax, lam, h0, pool_w, pool_b, pool_s, w_out,
              *, seq_len, tm):
    n, d = h.shape
    da = cw.shape[1]
    db = pool_w.shape[0]
    nt = n // tm
    ident = lambda i: i
    za_prev, za_next = _halo_specs(tm, da, 0, n, (CONV_A // 2) * SUBLANES, SUBLANES, ident)
    pool_halo = POOL_HALF[-1] * SUBLANES
    zb_col = (2 * da) // db
    zb_prev, zb_next = _halo_specs(tm, db, zb_col, n, pool_halo, pool_halo, ident)
    consts = (mod8, g_post, cw, cb, wax, bax, lam, h0, pool_w, pool_b, pool_s, w_out)
    return pl.pallas_call(
        functools.partial(_mix_kernel, seq_len=seq_len),
        grid=(nt,),
        in_specs=[
            pl.BlockSpec((tm, da), lambda i: (i, 0)), za_prev, za_next,
            pl.BlockSpec((tm, da), lambda i: (i, 1)),
            pl.BlockSpec((tm, db), lambda i: (i, zb_col)), zb_prev, zb_next,
            pl.BlockSpec((tm, yc.shape[1]), lambda i: (i, 0)),
            pl.BlockSpec((tm, da), lambda i: (i, 0)),
            pl.BlockSpec((tm, d), lambda i: (i, 0)),
        ] + [_const_spec(a.shape) for a in consts],
        out_specs=[
            pl.BlockSpec((tm, d), lambda i: (i, 0)),
            pl.BlockSpec((SUBLANES, da), lambda i: (0, 0)),
        ],
        out_shape=[
            jax.ShapeDtypeStruct((n, d), F32),
            jax.ShapeDtypeStruct((SUBLANES, da), F32),
        ],
        scratch_shapes=[
            pltpu.VMEM((tm, da), F32), pltpu.VMEM((tm, da), F32), pltpu.VMEM((SUBLANES, da), F32),
            pltpu.VMEM((tm, da), F32),
        ],
        compiler_params=pltpu.CompilerParams(
            dimension_semantics=("arbitrary",), vmem_limit_bytes=V7X_VMEM_LIMIT),
        name="token_mix",
    )(z, z, z, z, z, z, z, yc, hb, h, *consts)


def _tile_rows(x8, rows):
    return jnp.broadcast_to(x8[None], (rows // SUBLANES,) + x8.shape).reshape(rows, x8.shape[1])


def _ffn_grid_kernel(hrow_ref, mod_ref, gpre_ref, gpost_ref, wup_ref, cw_ref, cb_ref, wdown_ref,
                     out_ref, ring, hm_s, hres_s, zbuf0, zbuf1, ybuf0, ybuf1, gbuf, *, n_img_rows):
    zbuf = (zbuf0, zbuf1)
    ybuf = (ybuf0, ybuf1)
    s = pl.program_id(0)
    rm, d = hrow_ref.shape
    nch = wdown_ref.shape[0]
    nc = wdown_ref.shape[1]
    rb = FFN_ROW_BLOCK
    nb = rb // SUBLANES
    pad = SUBLANES

    @pl.when(s == 0)
    def _():
        for k in range(2 * nch):
            ring[0, k] = jnp.zeros((rm, nc), F32)
            ring[1, k] = _tile_rows(cb_ref[k], rm)
        hres_s[...] = hrow_ref[...]
        for zb in zbuf:
            zb[...] = jnp.zeros_like(zb)

    hm_s[...] = _modulate(hrow_ref[...], gpre_ref[...], _mod_slice(mod_ref, 3, d),
                          _mod_slice(mod_ref, 4, d)).astype(BF16)
    out_ref[...] = jnp.zeros_like(out_ref)
    has_row = (s < n_img_rows).astype(F32)

    def up_proj(c, half):
        zbuf[c % 2][half, pad:pad + rm] = _bdot(hm_s[...], wup_ref[half * nch + c])

    def down_proj(c):
        out_ref[...] += _bdot(ybuf[c % 2][...], wdown_ref[c])

    def conv_half(c, half):
        slot = c % 2
        k = half * nch + c
        for col in range(nc // LANES):
            ls = slice(col * LANES, (col + 1) * LANES)
            w = [cw_ref[k, j * SUBLANES:(j + 1) * SUBLANES, ls][None] for j in range(9)]
            w_row = [w[6 + j] * has_row for j in range(3)]
            bias = cb_ref[k, :, ls][None]

            def taps(zl, z, zr, w3):
                return zl * w3[0] + z * w3[1] + zr * w3[2]

            for r0 in range(0, rm, rb):
                rs = slice(r0, r0 + rb)
                win = zbuf[slot][half, r0:r0 + rb + 2 * pad, ls].reshape(nb + 2, SUBLANES, LANES)
                zl, z, zr = win[0:nb], win[1:nb + 1], win[2:nb + 2]
                shape3 = (nb, SUBLANES, LANES)
                done = ring[0, k, rs, ls].reshape(shape3) + taps(zl, z, zr, w_row)
                ring[0, k, rs, ls] = (ring[1, k, rs, ls].reshape(shape3) + taps(zl, z, zr, w[3:6])).reshape(rb, LANES)
                ring[1, k, rs, ls] = (bias + taps(zl, z, zr, w[0:3])).reshape(rb, LANES)
                done = done.reshape(rb, LANES)
                if half == 0:
                    gbuf[rs, ls] = _gelu(done)
                else:
                    ybuf[slot][rs, ls] = (gbuf[rs, ls] * done).astype(BF16)

    up_proj(0, 0)
    up_proj(0, 1)
    for c in range(nch):
        if c + 1 < nch:
            up_proj(c + 1, 0)
        if c > 0:
            down_proj(c - 1)
        conv_half(c, 0)
        if c + 1 < nch:
            up_proj(c + 1, 1)
        conv_half(c, 1)
    down_proj(nch - 1)
    out_ref[...] = hres_s[...] + _mul_b(_rms(out_ref[...], gpost_ref[...]), _mod_slice(mod_ref, 5, d))
    hres_s[...] = hrow_ref[...]


def _ffn_grid_call(h, mod8, g_pre, g_post, w_up, cw, cb, w_down, *, grid_w):
    n, d = h.shape
    rm = grid_w * SUBLANES
    n_img_rows = n // rm
    nch, nc, _ = w_down.shape
    consts = (mod8, g_pre, g_post, w_up, cw, cb, w_down)
    return pl.pallas_call(
        functools.partial(_ffn_grid_kernel, n_img_rows=n_img_rows),
        grid=(n_img_rows + 1,),
        in_specs=[
            pl.BlockSpec((rm, d), lambda s: (jnp.minimum(s, n_img_rows - 1), 0)),
        ] + [_const_spec(a.shape) for a in consts],
        out_specs=pl.BlockSpec((rm, d), lambda s: (jnp.maximum(s - 1, 0), 0)),
        out_shape=jax.ShapeDtypeStruct((n, d), F32),
        scratch_shapes=[
            pltpu.VMEM((2, 2 * nch, rm, nc), F32),
            pltpu.VMEM((rm, d), BF16),
            pltpu.VMEM((rm, d), F32),
            pltpu.VMEM((2, rm + 2 * SUBLANES, nc), F32),
            pltpu.VMEM((2, rm + 2 * SUBLANES, nc), F32),
            pltpu.VMEM((rm, nc), BF16),
            pltpu.VMEM((rm, nc), BF16),
            pltpu.VMEM((rm, nc), F32),
        ],
        compiler_params=pltpu.CompilerParams(
            dimension_semantics=("arbitrary",), vmem_limit_bytes=V7X_VMEM_LIMIT_MAX),
        name="conv_ffn_grid",
    )(h, *consts)


def _ffn_seq_kernel(h_ref, hp_ref, hn_ref, mod_ref, gpre_ref, gpost_ref, wup_ref, cw_ref, cb_ref, wdown_ref,
                    out_ref, hm_s, acc_s):
    i = pl.program_id(0)
    nt = pl.num_programs(0)
    tm, d = h_ref.shape
    nch = wdown_ref.shape[0]
    shift8 = _mod_slice(mod_ref, 3, d)
    scale8 = _mod_slice(mod_ref, 4, d)
    gpre = gpre_ref[...]
    has_prev = (i > 0).astype(F32)
    has_next = (i < nt - 1).astype(F32)
    hm_s[0:SUBLANES] = (_modulate(hp_ref[...], gpre, shift8, scale8) * has_prev).astype(BF16)
    hm_s[SUBLANES:SUBLANES + tm] = _modulate(h_ref[...], gpre, shift8, scale8).astype(BF16)
    hm_s[SUBLANES + tm:2 * SUBLANES + tm] = (_modulate(hn_ref[...], gpre, shift8, scale8) * has_next).astype(BF16)
    acc_s[...] = jnp.zeros_like(acc_s)

    def chunk(c, carry_unused):
        done = []
        for half in range(2):
            k = half * nch + c
            z = _bdot(hm_s[...], wup_ref[k])
            w = [cw_ref[k, j * SUBLANES:j * SUBLANES + 1, :] for j in (3, 4, 5)]
            done.append(cb_ref[k, 0:1, :] + z[0:tm] * w[0] + z[SUBLANES:SUBLANES + tm] * w[1]
                        + z[2 * SUBLANES:2 * SUBLANES + tm] * w[2])
        y = (_gelu(done[0]) * done[1]).astype(BF16)
        acc_s[...] += _bdot(y, wdown_ref[c])
        return carry_unused

    lax.fori_loop(0, nch, chunk, 0)
    out_ref[...] = h_ref[...] + _mul_b(_rms(acc_s[...], gpost_ref[...]), _mod_slice(mod_ref, 5, d))


def _ffn_seq_call(h, mod8, g_pre, g_post, w_up, cw, cb, w_down, *, tm):
    n, d = h.shape
    nt = n // tm
    prev, nxt = _halo_specs(tm, d, 0, n, SUBLANES, SUBLANES, lambda i: i)
    consts = (mod8, g_pre, g_post, w_up, cw, cb, w_down)
    return pl.pallas_call(
        _ffn_seq_kernel,
        grid=(nt,),
        in_specs=[pl.BlockSpec((tm, d), lambda i: (i, 0)), prev, nxt] + [_const_spec(a.shape) for a in consts],
        out_specs=pl.BlockSpec((tm, d), lambda i: (i, 0)),
        out_shape=jax.ShapeDtypeStruct((n, d), F32),
        scratch_shapes=[
            pltpu.VMEM((tm + 2 * SUBLANES, d), BF16),
            pltpu.VMEM((tm, d), F32),
        ],
        compiler_params=pltpu.CompilerParams(
            dimension_semantics=("arbitrary",), vmem_limit_bytes=V7X_VMEM_LIMIT),
        name="conv_ffn_seq",
    )(h, h, h, *consts)


def _to_rows(x):
    b, l, d = x.shape
    return jnp.transpose(x, (1, 0, 2)).reshape(l * b, d)


def _from_rows(h, b):
    n, d = h.shape
    return jnp.transpose(h.reshape(n // b, b, d), (1, 0, 2))


def _block_diag(w):
    hh, ii, jj = w.shape
    tiled = jnp.concatenate([w.reshape(hh * ii, jj)] * hh, axis=1)
    r = lax.broadcasted_iota(jnp.int32, tiled.shape, 0) // ii
    c = lax.broadcasted_iota(jnp.int32, tiled.shape, 1) // jj
    return jnp.where(r == c, tiled, 0.0)


def _kron_batch_identity(w):
    ii, jj = w.shape[-2:]
    rep_i = (lax.broadcasted_iota(jnp.int32, (ii * SUBLANES, ii), 0) // SUBLANES
             == lax.broadcasted_iota(jnp.int32, (ii * SUBLANES, ii), 1)).astype(BF16)
    rep_j = (lax.broadcasted_iota(jnp.int32, (jj * SUBLANES, jj), 0) // SUBLANES
             == lax.broadcasted_iota(jnp.int32, (jj * SUBLANES, jj), 1)).astype(BF16)
    rows = jnp.einsum('ri,...ij->...rj', rep_i, w.astype(BF16), preferred_element_type=F32).astype(BF16)
    full = jnp.einsum('...rj,cj->...rc', rows, rep_j, preferred_element_type=F32)
    same_b = (lax.broadcasted_iota(jnp.int32, full.shape[-2:], 0) % SUBLANES
              == lax.broadcasted_iota(jnp.int32, full.shape[-2:], 1) % SUBLANES)
    return jnp.where(same_b, full, 0.0).astype(BF16)


def kernel(x, c, ctx, c_ctx, w_mod, b_mod, g_pre_mix, g_post_mix, g_pre_ffn, g_post_ffn, w_in, conv_a_w, conv_a_b, lru_w_a, lru_b_a, lru_w_x, lru_b_x, lru_lam, pool_w, pool_b, pool_scale, gmlp_norm, gmlp_w_s, gmlp_b_s, w_out, ffn_w_up, ffn_conv_w, ffn_conv_b, ffn_w_down):
    batch, seq_len, d = x.shape
    ctx_len = ctx.shape[1]
    depth = w_mod.shape[0]
    assert batch == SUBLANES, "the (t, b) row layout puts the batch on the 8 sublanes"
    d_a = conv_a_w.shape[2]
    d_b = pool_b.shape[1]
    d_c = gmlp_norm.shape[1]
    d_ff = ffn_w_down.shape[1]
    assert gmlp_w_s.shape[2] == CHUNK and seq_len % CHUNK == 0 and ctx_len % CHUNK == 0
    assert seq_len % GRID_W == 0 and d_ff % FFN_NC == 0
    nch = d_ff // FFN_NC
    d_keep = 2 * d_a + d_b

    h_lat = _to_rows(x)
    h_ctx = _to_rows(ctx)

    pad = jnp.zeros((2 * SUBLANES - batch - 1, d), F32)
    c_all = jnp.concatenate([c, c_ctx[None], pad], axis=0)
    mod = _mod_call(c_all, w_mod, b_mod)
    zero_state = jnp.zeros((SUBLANES, d_a), F32)
    kron_all = _kron_batch_identity(gmlp_w_s)

    for l in range(depth):
        last = l == depth - 1
        mod_lat = mod[l, :batch]
        mod_ctx = jnp.broadcast_to(mod[l, batch:batch + 1], (SUBLANES, N_MOD * d))
        row = lambda v: v.reshape(1, -1)

        w_in_l = w_in[l].astype(BF16)
        kron = kron_all[l]
        bias_s = jnp.repeat(jnp.repeat(gmlp_b_s[l].T, SUBLANES, axis=0), d_c // N_GROUPS_C, axis=1)
        wax = [jnp.concatenate([_block_diag(lru_w_a[l, k]), _block_diag(lru_w_x[l, k])], axis=1).astype(BF16)
               for k in range(2)]
        bax = [jnp.concatenate([lru_b_a[l, k], lru_b_x[l, k]]).reshape(1, -1) for k in range(2)]
        lam = [row(lru_lam[l, k]) for k in range(2)]
        cw_a, cb_a = conv_a_w[l], row(conv_a_b[l])
        pool_w_l = _block_diag(pool_w[l]).astype(BF16)
        w_out_l = w_out[l].astype(BF16)
        w_up_l = jnp.transpose(ffn_w_up[l].reshape(d, 2 * nch, FFN_NC), (1, 0, 2)).astype(BF16)
        cw_f = jnp.repeat(jnp.transpose(ffn_conv_w[l].reshape(9, 2 * nch, FFN_NC), (1, 0, 2)), SUBLANES, axis=1)
        cb_f = jnp.broadcast_to(ffn_conv_b[l].reshape(2 * nch, 1, FFN_NC), (2 * nch, SUBLANES, FFN_NC))
        w_down_l = ffn_w_down[l].reshape(nch, FFN_NC, d).astype(BF16)

        def mix(h, mod8, h0_f, h0_b, length):
            z, yc = _in_call(h, mod8, row(g_pre_mix[l]), w_in_l, kron, row(gmlp_norm[l]), bias_s,
                             d_keep=d_keep, d_c=d_c)
            hb, state_b = _scan_call(z, cw_a, cb_a, wax[1], bax[1], lam[1], h0_b, reverse=True, tm=512)
            h_new, state_f = _mix_call(z, yc, hb, h, mod8, row(g_post_mix[l]), cw_a, cb_a, wax[0], bax[0],
                                       lam[0], h0_f, pool_w_l, row(pool_b[l]), row(pool_scale[l]), w_out_l,
                                       seq_len=length, tm=512)
            return h_new, state_f, state_b

        h_ctx_mixed, ctx_f, ctx_b = mix(h_ctx, mod_ctx, zero_state, zero_state, ctx_len)
        h_lat, _, _ = mix(h_lat, mod_lat, ctx_f, ctx_b, seq_len)

        ffn_w = (row(g_pre_ffn[l]), row(g_post_ffn[l]), w_up_l, cw_f, cb_f, w_down_l)
        h_lat = _ffn_grid_call(h_lat, mod_lat, *ffn_w, grid_w=GRID_W)
        if not last:
            h_ctx = _ffn_seq_call(h_ctx_mixed, mod_ctx, *ffn_w, tm=512)

    return _from_rows(h_lat, batch)
```

```python
import functools
import math

import jax
import jax.numpy as jnp
from jax import lax
from jax.experimental import pallas as pl
from jax.experimental.pallas import tpu as pltpu

GRID_W = 64
CONV_A = 4
LRU_C = 8.0
N_HEADS_A = 8
POOL_HALF = (1, 2, 4, 8)
N_GROUPS_C = 4
CHUNK = 128
N_MOD = 6
EPS = 1e-6
GELU_K = math.sqrt(2.0 / math.pi)
GELU_C = 0.044715

SUBLANES = 8
LANES = 128
V7X_VMEM_LIMIT = 56 * 1024 * 1024

V7X_VMEM_LIMIT_MAX = 60 * 1024 * 1024

SEQ_TILE = 1024
FFN_NC = 256
FFN_ROW_BLOCK = 32

F32 = jnp.float32
BF16 = jnp.bfloat16


def _sigmoid(x):
    return 0.5 * (jnp.tanh(0.5 * x) + 1.0)


def _gelu(x):
    th = jnp.tanh(x * (GELU_K + (GELU_K * GELU_C) * (x * x)))
    hx = 0.5 * x
    return hx + hx * th


def _rms(x, g):
    ms = jnp.mean(x * x, axis=-1, keepdims=True)
    return x * lax.rsqrt(ms + EPS) * g


def _bdot(a, b):
    return jnp.dot(a, b, preferred_element_type=F32)


def _mul_b(x, v8):
    r, c = x.shape
    return (x.reshape(r // SUBLANES, SUBLANES, c) * v8[None]).reshape(r, c)


def _add_b(x, v8):
    r, c = x.shape
    return (x.reshape(r // SUBLANES, SUBLANES, c) + v8[None]).reshape(r, c)


def _modulate(x, g, shift8, scale8):
    return _add_b(_mul_b(_rms(x, g), 1.0 + scale8), shift8)


def _mod_slice(mod_ref, k, d):
    return mod_ref[:, k * d:(k + 1) * d]


def _const_spec(shape):
    nd = len(shape)
    return pl.BlockSpec(shape, lambda *_: (0,) * nd, pipeline_mode=pl.Buffered(1))


def _mod_kernel(c_ref, w_ref, b_ref, o_ref):
    c = c_ref[...]
    s = c * _sigmoid(c)
    o_ref[0] = jnp.dot(s, w_ref[0], preferred_element_type=F32,
                       precision=lax.Precision.HIGHEST) + b_ref[0]


def _mod_call(c_all, w_mod, b_mod):
    depth, d, nm = w_mod.shape
    rows = c_all.shape[0]
    nt = 1536
    return pl.pallas_call(
        _mod_kernel,
        grid=(depth, nm // nt),
        in_specs=[
            pl.BlockSpec((rows, d), lambda l, j: (0, 0)),
            pl.BlockSpec((1, d, nt), lambda l, j: (l, 0, j)),
            pl.BlockSpec((1, 1, nt), lambda l, j: (l, 0, j)),
        ],
        out_specs=pl.BlockSpec((1, rows, nt), lambda l, j: (l, 0, j)),
        out_shape=jax.ShapeDtypeStruct((depth, rows, nm), F32),
        compiler_params=pltpu.CompilerParams(
            dimension_semantics=("arbitrary", "arbitrary"),
            vmem_limit_bytes=40 * 1024 * 1024),
        name="mod_vectors",
    )(c_all, w_mod, b_mod.reshape(depth, 1, nm))


def _in_kernel(h_ref, mod_ref, g_ref, w_ref, kron_ref, gn_ref, bs_ref, z_ref, yc_ref, *, d_keep, d_c):
    d = h_ref.shape[1]
    hm = _modulate(h_ref[...], g_ref[...], _mod_slice(mod_ref, 0, d), _mod_slice(mod_ref, 1, d))
    z = _bdot(hm.astype(BF16), w_ref[...])
    z_ref[...] = z[:, :d_keep]
    zc = _gelu(z[:, d_keep:])
    u = zc[:, :d_c]
    v = zc[:, d_c:]
    mu = jnp.mean(v, axis=-1, keepdims=True)
    vc = v - mu
    var = jnp.mean(vc * vc, axis=-1, keepdims=True)
    vn = (vc * lax.rsqrt(var + EPS) * gn_ref[...]).astype(BF16)
    group = lax.broadcasted_iota(jnp.int32, (1, d_c), 1) // (d_c // N_GROUPS_C)
    s = bs_ref[...]
    for g in range(N_GROUPS_C):
        s = s + jnp.where(group == g, _bdot(kron_ref[g], vn), 0.0)
    yc_ref[...] = u * s


def _in_call(h, mod8, g_pre, w_in, kron, gnorm, bias_s, *, d_keep, d_c):
    n, d = h.shape
    tm = CHUNK * SUBLANES
    d_in = w_in.shape[1]
    return pl.pallas_call(
        functools.partial(_in_kernel, d_keep=d_keep, d_c=d_c),
        grid=(n // tm,),
        in_specs=[
            pl.BlockSpec((tm, d), lambda i: (i, 0)),
            _const_spec(mod8.shape),
            _const_spec(g_pre.shape),
            _const_spec(w_in.shape),
            _const_spec(kron.shape),
            _const_spec(gnorm.shape),
            _const_spec(bias_s.shape),
        ],
        out_specs=[
            pl.BlockSpec((tm, d_keep), lambda i: (i, 0)),
            pl.BlockSpec((tm, d_c), lambda i: (i, 0)),
        ],
        out_shape=[
            jax.ShapeDtypeStruct((n, d_keep), F32),
            jax.ShapeDtypeStruct((n, d_c), F32),
        ],
        compiler_params=pltpu.CompilerParams(
            dimension_semantics=("arbitrary",), vmem_limit_bytes=V7X_VMEM_LIMIT),
        name="mix_in_proj",
    )(h, mod8, g_pre, w_in, kron, gnorm, bias_s)


def _rglru_tile(zm_ref, zp_ref, zn_ref, has_prev, has_next, cw_ref, cb_ref, wax_ref, bax_ref, lam_ref,
                a_s, b_s, carry, hout_ref, *, reverse):
    tm, da = zm_ref.shape
    ext = jnp.concatenate([zp_ref[...] * has_prev, zm_ref[...], zn_ref[...] * has_next], axis=0)
    cw = cw_ref[...]
    xa = cb_ref[...]
    for k in range(CONV_A):
        xa = xa + ext[k * SUBLANES:k * SUBLANES + tm] * cw[k:k + 1]
    pre = _bdot(xa.astype(BF16), wax_ref[...]) + bax_ref[...]
    r = _sigmoid(pre[:, :da])
    gate_i = _sigmoid(pre[:, da:])
    lam = lam_ref[...]
    softplus_neg_lam = jnp.maximum(-lam, 0.0) + jnp.log1p(jnp.exp(-jnp.abs(lam)))
    log_a = (-LRU_C * r) * softplus_neg_lam
    a = jnp.exp(log_a)
    one_minus_a2 = -jnp.tanh(log_a) * (a * a + 1.0)
    a_s[...] = a
    b_s[...] = jnp.sqrt(one_minus_a2) * (gate_i * xa)
    h = carry[...]
    steps = tm // SUBLANES
    order = range(steps - 1, -1, -1) if reverse else range(steps)
    for t in order:
        sl = pl.ds(t * SUBLANES, SUBLANES)
        h = a_s[sl, :] * h + b_s[sl, :]
        hout_ref[sl, :] = h
    carry[...] = h


def _scan_kernel(zm_ref, zp_ref, zn_ref, cw_ref, cb_ref, wax_ref, bax_ref, lam_ref, h0_ref,
                 hout_ref, state_ref, a_s, b_s, carry, *, reverse):
    i = pl.program_id(0)
    nt = pl.num_programs(0)
    tile = nt - 1 - i if reverse else i

    @pl.when(i == 0)
    def _():
        carry[...] = h0_ref[...]

    has_prev = (tile > 0).astype(F32)
    has_next = (tile < nt - 1).astype(F32)
    _rglru_tile(zm_ref, zp_ref, zn_ref, has_prev, has_next, cw_ref, cb_ref, wax_ref, bax_ref, lam_ref,
                a_s, b_s, carry, hout_ref, reverse=reverse)
    state_ref[...] = carry[...]


def _halo_specs(tm, width, col_block, n_rows, before, after, tile_of):
    pb = tm // before
    nb = tm // after
    last_nb = n_rows // after - 1
    prev = pl.BlockSpec((before, width), lambda i: (jnp.maximum(tile_of(i) * pb - 1, 0), col_block))
    nxt = pl.BlockSpec((after, width), lambda i: (jnp.minimum((tile_of(i) + 1) * nb, last_nb), col_block))
    return prev, nxt


def _scan_call(z, cw, cb, wax, bax, lam, h0, *, reverse, tm):
    n = z.shape[0]
    da = cw.shape[1]
    nt = n // tm
    tile_of = (lambda i: nt - 1 - i) if reverse else (lambda i: i)
    prev, nxt = _halo_specs(tm, da, 0, n, (CONV_A // 2) * SUBLANES, SUBLANES, tile_of)
    return pl.pallas_call(
        functools.partial(_scan_kernel, reverse=reverse),
        grid=(nt,),
        in_specs=[
            pl.BlockSpec((tm, da), lambda i: (tile_of(i), 0)), prev, nxt,
            _const_spec(cw.shape), _const_spec(cb.shape), _const_spec(wax.shape),
            _const_spec(bax.shape), _const_spec(lam.shape), _const_spec(h0.shape),
        ],
        out_specs=[
            pl.BlockSpec((tm, da), lambda i: (tile_of(i), 0)),
            pl.BlockSpec((SUBLANES, da), lambda i: (0, 0)),
        ],
        out_shape=[
            jax.ShapeDtypeStruct((n, da), F32),
            jax.ShapeDtypeStruct((SUBLANES, da), F32),
        ],
        scratch_shapes=[
            pltpu.VMEM((tm, da), F32), pltpu.VMEM((tm, da), F32), pltpu.VMEM((SUBLANES, da), F32),
        ],
        compiler_params=pltpu.CompilerParams(
            dimension_semantics=("arbitrary",), vmem_limit_bytes=32 * 1024 * 1024),
        name="rglru_scan_bwd" if reverse else "rglru_scan_fwd",
    )(z, z, z, cw, cb, wax, bax, lam, h0)


def _pool_means(ext, zmain_rows, t0, seq_len):
    halo = POOL_HALF[-1]
    e_rows, width = ext.shape
    e = e_rows // SUBLANES
    tm = zmain_rows

    def rows(arr, start, count):
        return arr[start * SUBLANES:(start + count) * SUBLANES]

    s2 = rows(ext, 0, e - 1) + rows(ext, 1, e - 1)
    s4 = rows(s2, 0, e - 3) + rows(s2, 2, e - 3)
    s8 = rows(s4, 0, e - 7) + rows(s4, 4, e - 7)
    s16 = rows(s8, 0, e - 15) + rows(s8, 8, e - 15)
    t = tm // SUBLANES
    sums = (rows(s2, halo - 1, t), rows(s4, halo - 2, t), rows(s8, halo - 4, t), rows(s16, 0, t))
    group = lax.broadcasted_iota(jnp.int32, (1, width), 1) // (width // len(POOL_HALF))
    win = sums[-1]
    half = jnp.full((1, width), POOL_HALF[-1], jnp.int32)
    for g in range(len(POOL_HALF) - 2, -1, -1):
        win = jnp.where(group == g, sums[g], win)
        half = jnp.where(group == g, POOL_HALF[g], half)
    pos = t0 + lax.broadcasted_iota(jnp.int32, (tm, width), 0) // SUBLANES
    lo = jnp.maximum(pos - half, 0)
    hi = jnp.minimum(pos + half, seq_len)
    return win / (hi - lo).astype(F32)


def _mix_kernel(za_ref, zap_ref, zan_ref, zg_ref, zb_ref, zbp_ref, zbn_ref, yc_ref, hb_ref, h_ref,
                mod_ref, gpost_ref, cw_ref, cb_ref, wax_ref, bax_ref, lam_ref, h0_ref,
                pw_ref, pb_ref, ps_ref, wout_ref,
                out_ref, state_ref, a_s, b_s, carry, hf_s, *, seq_len):
    i = pl.program_id(0)
    nt = pl.num_programs(0)
    tm, d = h_ref.shape

    @pl.when(i == 0)
    def _():
        carry[...] = h0_ref[...]

    has_prev = (i > 0).astype(F32)
    has_next = (i < nt - 1).astype(F32)
    _rglru_tile(za_ref, zap_ref, zan_ref, has_prev, has_next, cw_ref, cb_ref, wax_ref, bax_ref, lam_ref,
                a_s, b_s, carry, hf_s, reverse=False)
    state_ref[...] = carry[...]

    ya = _gelu(zg_ref[...]) * (hf_s[...] + hb_ref[...])

    zb = zb_ref[...]
    ext = jnp.concatenate([zbp_ref[...] * has_prev, zb, zbn_ref[...] * has_next], axis=0)
    t0 = i * (tm // SUBLANES)
    p = _pool_means(ext, tm, t0, seq_len) - zb
    yb = (_bdot(p.astype(BF16), pw_ref[...]) + pb_ref[...]) * ps_ref[...]

    cat = jnp.concatenate([ya, yb, yc_ref[...]], axis=1).astype(BF16)
    mix = _bdot(cat, wout_ref[...])
    out_ref[...] = h_ref[...] + _mul_b(_rms(mix, gpost_ref[...]), _mod_slice(mod_ref, 2, d))


def _mix_call(z, yc, hb, h, mod8, g_post, cw, cb, wax, bax, lam, h0, pool_w, pool_b, pool_s, w_out,
              *, seq_len, tm):
    n, d = h.shape
    da = cw.shape[1]
    db = pool_w.shape[0]
    nt = n // tm
    ident = lambda i: i
    za_prev, za_next = _halo_specs(tm, da, 0, n, (CONV_A // 2) * SUBLANES, SUBLANES, ident)
    pool_halo = POOL_HALF[-1] * SUBLANES
    zb_col = (2 * da) // db
    zb_prev, zb_next = _halo_specs(tm, db, zb_col, n, pool_halo, pool_halo, ident)
    consts = (mod8, g_post, cw, cb, wax, bax, lam, h0, pool_w, pool_b, pool_s, w_out)
    return pl.pallas_call(
        functools.partial(_mix_kernel, seq_len=seq_len),
        grid=(nt,),
        in_specs=[
            pl.BlockSpec((tm, da), lambda i: (i, 0)), za_prev, za_next,
            pl.BlockSpec((tm, da), lambda i: (i, 1)),
            pl.BlockSpec((tm, db), lambda i: (i, zb_col)), zb_prev, zb_next,
            pl.BlockSpec((tm, yc.shape[1]), lambda i: (i, 0)),
            pl.BlockSpec((tm, da), lambda i: (i, 0)),
            pl.BlockSpec((tm, d), lambda i: (i, 0)),
        ] + [_const_spec(a.shape) for a in consts],
        out_specs=[
            pl.BlockSpec((tm, d), lambda i: (i, 0)),
            pl.BlockSpec((SUBLANES, da), lambda i: (0, 0)),
        ],
        out_shape=[
            jax.ShapeDtypeStruct((n, d), F32),
            jax.ShapeDtypeStruct((SUBLANES, da), F32),
        ],
        scratch_shapes=[
            pltpu.VMEM((tm, da), F32), pltpu.VMEM((tm, da), F32), pltpu.VMEM((SUBLANES, da), F32),
            pltpu.VMEM((tm, da), F32),
        ],
        compiler_params=pltpu.CompilerParams(
            dimension_semantics=("arbitrary",), vmem_limit_bytes=V7X_VMEM_LIMIT),
        name="token_mix",
    )(z, z, z, z, z, z, z, yc, hb, h, *consts)


def _tile_rows(x8, rows):
    return jnp.broadcast_to(x8[None], (rows // SUBLANES,) + x8.shape).reshape(rows, x8.shape[1])


def _ffn_grid_kernel(hrow_ref, mod_ref, gpre_ref, gpost_ref, wup_ref, cw_ref, cb_ref, wdown_ref,
                     out_ref, ring, hm_s, hres_s, zbuf0, zbuf1, ybuf0, ybuf1, gbuf, *slabs,
                     n_img_rows, batch_major_out):
    zbuf = (zbuf0, zbuf1)
    ybuf = (ybuf0, ybuf1)
    s = pl.program_id(0)
    rm, d = hrow_ref.shape
    n_slabs = d // LANES

    if batch_major_out:
        acc3, = slabs

        def acc_zero():
            acc3[...] = jnp.zeros_like(acc3)

        def acc_add(r):
            for j in range(n_slabs):
                acc3[j] += r[:, j * LANES:(j + 1) * LANES]

        def acc_get():
            return jnp.concatenate([acc3[j] for j in range(n_slabs)], axis=1)

        def emit(res):
            for j in range(n_slabs):
                acc3[j] = res[:, j * LANES:(j + 1) * LANES]
            for j in range(n_slabs):
                for b in range(SUBLANES):
                    out_ref[b, :, j * LANES:(j + 1) * LANES] = acc3[j, pl.ds(b, rm // SUBLANES, stride=SUBLANES), :]
    else:
        def acc_zero():
            out_ref[...] = jnp.zeros_like(out_ref)

        def acc_add(r):
            out_ref[...] += r

        def acc_get():
            return out_ref[...]

        def emit(res):
            out_ref[...] = res
    nch = wdown_ref.shape[0]
    nc = wdown_ref.shape[1]
    rb = FFN_ROW_BLOCK
    nb = rb // SUBLANES
    pad = SUBLANES

    @pl.when(s == 0)
    def _():
        for k in range(2 * nch):
            ring[0, k] = jnp.zeros((rm, nc), F32)
            ring[1, k] = _tile_rows(cb_ref[k], rm)
        hres_s[...] = hrow_ref[...]
        for zb in zbuf:
            zb[...] = jnp.zeros_like(zb)

    hm_s[...] = _modulate(hrow_ref[...], gpre_ref[...], _mod_slice(mod_ref, 3, d),
                          _mod_slice(mod_ref, 4, d)).astype(BF16)
    acc_zero()
    has_row = (s < n_img_rows).astype(F32)

    def up_proj(c, half):
        zbuf[c % 2][half, pad:pad + rm] = _bdot(hm_s[...], wup_ref[half * nch + c])

    def down_proj(c):
        acc_add(_bdot(ybuf[c % 2][...], wdown_ref[c]))

    def conv_half(c, half):
        slot = c % 2
        k = half * nch + c
        for col in range(nc // LANES):
            ls = slice(col * LANES, (col + 1) * LANES)
            w = [cw_ref[k, j * SUBLANES:(j + 1) * SUBLANES, ls][None] for j in range(9)]
            w_row = [w[6 + j] * has_row for j in range(3)]
            bias = cb_ref[k, :, ls][None]

            def taps(zl, z, zr, w3):
                return zl * w3[0] + z * w3[1] + zr * w3[2]

            for r0 in range(0, rm, rb):
                rs = slice(r0, r0 + rb)
                win = zbuf[slot][half, r0:r0 + rb + 2 * pad, ls].reshape(nb + 2, SUBLANES, LANES)
                zl, z, zr = win[0:nb], win[1:nb + 1], win[2:nb + 2]
                shape3 = (nb, SUBLANES, LANES)
                done = ring[0, k, rs, ls].reshape(shape3) + taps(zl, z, zr, w_row)
                ring[0, k, rs, ls] = (ring[1, k, rs, ls].reshape(shape3) + taps(zl, z, zr, w[3:6])).reshape(rb, LANES)
                ring[1, k, rs, ls] = (bias + taps(zl, z, zr, w[0:3])).reshape(rb, LANES)
                done = done.reshape(rb, LANES)
                if half == 0:
                    gbuf[rs, ls] = _gelu(done)
                else:
                    ybuf[slot][rs, ls] = (gbuf[rs, ls] * done).astype(BF16)

    up_proj(0, 0)
    up_proj(0, 1)
    for c in range(nch):
        if c + 1 < nch:
            up_proj(c + 1, 0)
        if c > 0:
            down_proj(c - 1)
        conv_half(c, 0)
        if c + 1 < nch:
            up_proj(c + 1, 1)
        conv_half(c, 1)
    down_proj(nch - 1)
    emit(hres_s[...] + _mul_b(_rms(acc_get(), gpost_ref[...]), _mod_slice(mod_ref, 5, d)))
    hres_s[...] = hrow_ref[...]


def _ffn_grid_call(h, mod8, g_pre, g_post, w_up, cw, cb, w_down, *, grid_w, batch_major_out):
    n, d = h.shape
    rm = grid_w * SUBLANES
    n_img_rows = n // rm
    nch, nc, _ = w_down.shape
    consts = (mod8, g_pre, g_post, w_up, cw, cb, w_down)
    if batch_major_out:
        out_spec = pl.BlockSpec((SUBLANES, grid_w, d), lambda s: (0, jnp.maximum(s - 1, 0), 0))
        out_shape = jax.ShapeDtypeStruct((SUBLANES, n // SUBLANES, d), F32)
        slabs = [pltpu.VMEM((d // LANES, rm, LANES), F32)]
    else:
        out_spec = pl.BlockSpec((rm, d), lambda s: (jnp.maximum(s - 1, 0), 0))
        out_shape = jax.ShapeDtypeStruct((n, d), F32)
        slabs = []
    return pl.pallas_call(
        functools.partial(_ffn_grid_kernel, n_img_rows=n_img_rows, batch_major_out=batch_major_out),
        grid=(n_img_rows + 1,),
        in_specs=[
            pl.BlockSpec((rm, d), lambda s: (jnp.minimum(s, n_img_rows - 1), 0)),
        ] + [_const_spec(a.shape) for a in consts],
        out_specs=out_spec,
        out_shape=out_shape,
        scratch_shapes=[
            pltpu.VMEM((2, 2 * nch, rm, nc), F32),
            pltpu.VMEM((rm, d), BF16),
            pltpu.VMEM((rm, d), F32),
            pltpu.VMEM((2, rm + 2 * SUBLANES, nc), F32),
            pltpu.VMEM((2, rm + 2 * SUBLANES, nc), F32),
            pltpu.VMEM((rm, nc), BF16),
            pltpu.VMEM((rm, nc), BF16),
            pltpu.VMEM((rm, nc), F32),
        ] + slabs,
        compiler_params=pltpu.CompilerParams(
            dimension_semantics=("arbitrary",), vmem_limit_bytes=V7X_VMEM_LIMIT_MAX),
        name="conv_ffn_grid_out" if batch_major_out else "conv_ffn_grid",
    )(h, *consts)


def _ffn_seq_kernel(h_ref, hp_ref, hn_ref, mod_ref, gpre_ref, gpost_ref, wup_ref, cw_ref, cb_ref, wdown_ref,
                    out_ref, hm_s, acc_s):
    i = pl.program_id(0)
    nt = pl.num_programs(0)
    tm, d = h_ref.shape
    nch = wdown_ref.shape[0]
    shift8 = _mod_slice(mod_ref, 3, d)
    scale8 = _mod_slice(mod_ref, 4, d)
    gpre = gpre_ref[...]
    has_prev = (i > 0).astype(F32)
    has_next = (i < nt - 1).astype(F32)
    hm_s[0:SUBLANES] = (_modulate(hp_ref[...], gpre, shift8, scale8) * has_prev).astype(BF16)
    hm_s[SUBLANES:SUBLANES + tm] = _modulate(h_ref[...], gpre, shift8, scale8).astype(BF16)
    hm_s[SUBLANES + tm:2 * SUBLANES + tm] = (_modulate(hn_ref[...], gpre, shift8, scale8) * has_next).astype(BF16)
    acc_s[...] = jnp.zeros_like(acc_s)

    def chunk(c, carry_unused):
        done = []
        for half in range(2):
            k = half * nch + c
            z = _bdot(hm_s[...], wup_ref[k])
            w = [cw_ref[k, j * SUBLANES:j * SUBLANES + 1, :] for j in (3, 4, 5)]
            done.append(cb_ref[k, 0:1, :] + z[0:tm] * w[0] + z[SUBLANES:SUBLANES + tm] * w[1]
                        + z[2 * SUBLANES:2 * SUBLANES + tm] * w[2])
        y = (_gelu(done[0]) * done[1]).astype(BF16)
        acc_s[...] += _bdot(y, wdown_ref[c])
        return carry_unused

    lax.fori_loop(0, nch, chunk, 0)
    out_ref[...] = h_ref[...] + _mul_b(_rms(acc_s[...], gpost_ref[...]), _mod_slice(mod_ref, 5, d))


def _ffn_seq_call(h, mod8, g_pre, g_post, w_up, cw, cb, w_down, *, tm):
    n, d = h.shape
    nt = n // tm
    prev, nxt = _halo_specs(tm, d, 0, n, SUBLANES, SUBLANES, lambda i: i)
    consts = (mod8, g_pre, g_post, w_up, cw, cb, w_down)
    return pl.pallas_call(
        _ffn_seq_kernel,
        grid=(nt,),
        in_specs=[pl.BlockSpec((tm, d), lambda i: (i, 0)), prev, nxt] + [_const_spec(a.shape) for a in consts],
        out_specs=pl.BlockSpec((tm, d), lambda i: (i, 0)),
        out_shape=jax.ShapeDtypeStruct((n, d), F32),
        scratch_shapes=[
            pltpu.VMEM((tm + 2 * SUBLANES, d), BF16),
            pltpu.VMEM((tm, d), F32),
        ],
        compiler_params=pltpu.CompilerParams(
            dimension_semantics=("arbitrary",), vmem_limit_bytes=V7X_VMEM_LIMIT),
        name="conv_ffn_seq",
    )(h, h, h, *consts)


def _to_rows(x):
    b, l, d = x.shape
    return jnp.transpose(x, (1, 0, 2)).reshape(l * b, d)


def _block_diag(w):
    hh, ii, jj = w.shape
    tiled = jnp.concatenate([w.reshape(hh * ii, jj)] * hh, axis=1)
    r = lax.broadcasted_iota(jnp.int32, tiled.shape, 0) // ii
    c = lax.broadcasted_iota(jnp.int32, tiled.shape, 1) // jj
    return jnp.where(r == c, tiled, 0.0)


def _kron_batch_identity(w):
    ii, jj = w.shape[-2:]
    rep_i = (lax.broadcasted_iota(jnp.int32, (ii * SUBLANES, ii), 0) // SUBLANES
             == lax.broadcasted_iota(jnp.int32, (ii * SUBLANES, ii), 1)).astype(BF16)
    rep_j = (lax.broadcasted_iota(jnp.int32, (jj * SUBLANES, jj), 0) // SUBLANES
             == lax.broadcasted_iota(jnp.int32, (jj * SUBLANES, jj), 1)).astype(BF16)
    rows = jnp.einsum('ri,...ij->...rj', rep_i, w.astype(BF16), preferred_element_type=F32).astype(BF16)
    full = jnp.einsum('...rj,cj->...rc', rows, rep_j, preferred_element_type=F32)
    same_b = (lax.broadcasted_iota(jnp.int32, full.shape[-2:], 0) % SUBLANES
              == lax.broadcasted_iota(jnp.int32, full.shape[-2:], 1) % SUBLANES)
    return jnp.where(same_b, full, 0.0).astype(BF16)


def kernel(x, c, ctx, c_ctx, w_mod, b_mod, g_pre_mix, g_post_mix, g_pre_ffn, g_post_ffn, w_in, conv_a_w, conv_a_b, lru_w_a, lru_b_a, lru_w_x, lru_b_x, lru_lam, pool_w, pool_b, pool_scale, gmlp_norm, gmlp_w_s, gmlp_b_s, w_out, ffn_w_up, ffn_conv_w, ffn_conv_b, ffn_w_down):
    batch, seq_len, d = x.shape
    ctx_len = ctx.shape[1]
    depth = w_mod.shape[0]
    assert batch == SUBLANES, "the (t, b) row layout puts the batch on the 8 sublanes"
    d_a = conv_a_w.shape[2]
    d_b = pool_b.shape[1]
    d_c = gmlp_norm.shape[1]
    d_ff = ffn_w_down.shape[1]
    assert gmlp_w_s.shape[2] == CHUNK and seq_len % CHUNK == 0 and ctx_len % CHUNK == 0
    assert seq_len % GRID_W == 0 and d_ff % FFN_NC == 0
    nch = d_ff // FFN_NC
    d_keep = 2 * d_a + d_b

    h_lat = _to_rows(x)
    h_ctx = _to_rows(ctx)

    pad = jnp.zeros((2 * SUBLANES - batch - 1, d), F32)
    c_all = jnp.concatenate([c, c_ctx[None], pad], axis=0)
    mod = _mod_call(c_all, w_mod, b_mod)
    zero_state = jnp.zeros((SUBLANES, d_a), F32)
    kron_all = _kron_batch_identity(gmlp_w_s)

    for l in range(depth):
        last = l == depth - 1
        mod_lat = mod[l, :batch]
        mod_ctx = jnp.broadcast_to(mod[l, batch:batch + 1], (SUBLANES, N_MOD * d))
        row = lambda v: v.reshape(1, -1)

        w_in_l = w_in[l].astype(BF16)
        kron = kron_all[l]
        bias_s = jnp.repeat(jnp.repeat(gmlp_b_s[l].T, SUBLANES, axis=0), d_c // N_GROUPS_C, axis=1)
        wax = [jnp.concatenate([_block_diag(lru_w_a[l, k]), _block_diag(lru_w_x[l, k])], axis=1).astype(BF16)
               for k in range(2)]
        bax = [jnp.concatenate([lru_b_a[l, k], lru_b_x[l, k]]).reshape(1, -1) for k in range(2)]
        lam = [row(lru_lam[l, k]) for k in range(2)]
        cw_a, cb_a = conv_a_w[l], row(conv_a_b[l])
        pool_w_l = _block_diag(pool_w[l]).astype(BF16)
        w_out_l = w_out[l].astype(BF16)
        w_up_l = jnp.transpose(ffn_w_up[l].reshape(d, 2 * nch, FFN_NC), (1, 0, 2)).astype(BF16)
        cw_f = jnp.repeat(jnp.transpose(ffn_conv_w[l].reshape(9, 2 * nch, FFN_NC), (1, 0, 2)), SUBLANES, axis=1)
        cb_f = jnp.broadcast_to(ffn_conv_b[l].reshape(2 * nch, 1, FFN_NC), (2 * nch, SUBLANES, FFN_NC))
        w_down_l = ffn_w_down[l].reshape(nch, FFN_NC, d).astype(BF16)

        def mix(h, mod8, h0_f, h0_b, length):
            z, yc = _in_call(h, mod8, row(g_pre_mix[l]), w_in_l, kron, row(gmlp_norm[l]), bias_s,
                             d_keep=d_keep, d_c=d_c)
            hb, state_b = _scan_call(z, cw_a, cb_a, wax[1], bax[1], lam[1], h0_b, reverse=True, tm=SEQ_TILE)
            h_new, state_f = _mix_call(z, yc, hb, h, mod8, row(g_post_mix[l]), cw_a, cb_a, wax[0], bax[0],
                                       lam[0], h0_f, pool_w_l, row(pool_b[l]), row(pool_scale[l]), w_out_l,
                                       seq_len=length, tm=SEQ_TILE)
            return h_new, state_f, state_b

        h_ctx_mixed, ctx_f, ctx_b = mix(h_ctx, mod_ctx, zero_state, zero_state, ctx_len)
        h_lat, _, _ = mix(h_lat, mod_lat, ctx_f, ctx_b, seq_len)

        ffn_w = (row(g_pre_ffn[l]), row(g_post_ffn[l]), w_up_l, cw_f, cb_f, w_down_l)
        h_lat = _ffn_grid_call(h_lat, mod_lat, *ffn_w, grid_w=GRID_W, batch_major_out=last)
        if not last:
            h_ctx = _ffn_seq_call(h_ctx_mixed, mod_ctx, *ffn_w, tm=GRID_W * SUBLANES)

    return h_lat
```

```python
import functools
import math

import jax
import jax.numpy as jnp
from jax import lax
from jax.experimental import pallas as pl
from jax.experimental.pallas import tpu as pltpu

GRID_W = 64
CONV_A = 4
LRU_C = 8.0
N_HEADS_A = 8
POOL_HALF = (1, 2, 4, 8)
N_GROUPS_C = 4
CHUNK = 128
N_MOD = 6
EPS = 1e-6
GELU_K = math.sqrt(2.0 / math.pi)
GELU_C = 0.044715

SUBLANES = 8
LANES = 128
V7X_VMEM_LIMIT = 56 * 1024 * 1024

V7X_VMEM_LIMIT_MAX = 62 * 1024 * 1024

SEQ_TILE = 1024
FFN_NC = 256
FFN_DOWN_GROUP = 2
FFN_ROW_BLOCK = 32

F32 = jnp.float32
BF16 = jnp.bfloat16


def _sigmoid(x):
    return 0.5 * (jnp.tanh(0.5 * x) + 1.0)


def _gelu(x):
    th = jnp.tanh(x * (GELU_K + (GELU_K * GELU_C) * (x * x)))
    hx = 0.5 * x
    return hx + hx * th


def _rms(x, g):
    ms = jnp.mean(x * x, axis=-1, keepdims=True)
    return x * lax.rsqrt(ms + EPS) * g


def _bdot(a, b):
    return jnp.dot(a, b, preferred_element_type=F32)


def _mul_b(x, v8):
    r, c = x.shape
    return (x.reshape(r // SUBLANES, SUBLANES, c) * v8[None]).reshape(r, c)


def _add_b(x, v8):
    r, c = x.shape
    return (x.reshape(r // SUBLANES, SUBLANES, c) + v8[None]).reshape(r, c)


def _modulate(x, g, shift8, scale8):
    return _add_b(_mul_b(_rms(x, g), 1.0 + scale8), shift8)


def _mod_slice(mod_ref, k, d):
    return mod_ref[:, k * d:(k + 1) * d]


def _const_spec(shape):
    nd = len(shape)
    return pl.BlockSpec(shape, lambda *_: (0,) * nd, pipeline_mode=pl.Buffered(1))


def _mod_kernel(c_ref, w_ref, b_ref, o_ref):
    c = c_ref[...]
    s = c * _sigmoid(c)
    o_ref[0] = jnp.dot(s, w_ref[0], preferred_element_type=F32,
                       precision=lax.Precision.HIGHEST) + b_ref[0]


def _mod_call(c_all, w_mod, b_mod):
    depth, d, nm = w_mod.shape
    rows = c_all.shape[0]
    nt = 1536
    return pl.pallas_call(
        _mod_kernel,
        grid=(depth, nm // nt),
        in_specs=[
            pl.BlockSpec((rows, d), lambda l, j: (0, 0)),
            pl.BlockSpec((1, d, nt), lambda l, j: (l, 0, j)),
            pl.BlockSpec((1, 1, nt), lambda l, j: (l, 0, j)),
        ],
        out_specs=pl.BlockSpec((1, rows, nt), lambda l, j: (l, 0, j)),
        out_shape=jax.ShapeDtypeStruct((depth, rows, nm), F32),
        compiler_params=pltpu.CompilerParams(
            dimension_semantics=("arbitrary", "arbitrary"),
            vmem_limit_bytes=40 * 1024 * 1024),
        name="mod_vectors",
    )(c_all, w_mod, b_mod.reshape(depth, 1, nm))


def _in_kernel(h_ref, mod_ref, g_ref, w_ref, ws_ref, gn_ref, bs_ref, z_ref, yc_ref, vn_s, sg_s, *, d_keep, d_c):
    d = h_ref.shape[1]
    tm = h_ref.shape[0]
    npos = tm // SUBLANES
    hm = _modulate(h_ref[...], g_ref[...], _mod_slice(mod_ref, 0, d), _mod_slice(mod_ref, 1, d))
    z = _bdot(hm.astype(BF16), w_ref[...])
    z_ref[...] = z[:, :d_keep]
    zc = _gelu(z[:, d_keep:])
    u = zc[:, :d_c]
    v = zc[:, d_c:]
    mu = jnp.mean(v, axis=-1, keepdims=True)
    vc = v - mu
    var = jnp.mean(vc * vc, axis=-1, keepdims=True)
    vn = vc * lax.rsqrt(var + EPS) * gn_ref[...]
    n_slabs = d_c // LANES
    for j in range(n_slabs):
        vn_s[j] = vn[:, j * LANES:(j + 1) * LANES]
    group = lax.broadcasted_iota(jnp.int32, (1, d_c), 1) // (d_c // N_GROUPS_C)
    rows_of = [pl.ds(b, npos, stride=SUBLANES) for b in range(SUBLANES)]
    v_all = jnp.concatenate([vn_s[j, rows_of[b], :] for b in range(SUBLANES) for j in range(n_slabs)],
                            axis=1).astype(BF16)
    mixed = _bdot(ws_ref[...], v_all)
    for b in range(SUBLANES):
        sb = bs_ref[...]
        for g in range(N_GROUPS_C):
            sb = sb + jnp.where(group == g, mixed[g * npos:(g + 1) * npos, b * d_c:(b + 1) * d_c], 0.0)
        for j in range(n_slabs):
            sg_s[j, rows_of[b], :] = sb[:, j * LANES:(j + 1) * LANES]
    s = jnp.concatenate([sg_s[j] for j in range(n_slabs)], axis=1)
    yc_ref[...] = u * s


def _in_call(h, mod8, g_pre, w_in, w_s, gnorm, bias_s, *, d_keep, d_c):
    n, d = h.shape
    tm = CHUNK * SUBLANES
    slab = pltpu.VMEM((d_c // LANES, tm, LANES), F32)
    return pl.pallas_call(
        functools.partial(_in_kernel, d_keep=d_keep, d_c=d_c),
        grid=(n // tm,),
        in_specs=[
            pl.BlockSpec((tm, d), lambda i: (i, 0)),
            _const_spec(mod8.shape),
            _const_spec(g_pre.shape),
            _const_spec(w_in.shape),
            _const_spec(w_s.shape),
            _const_spec(gnorm.shape),
            _const_spec(bias_s.shape),
        ],
        out_specs=[
            pl.BlockSpec((tm, d_keep), lambda i: (i, 0)),
            pl.BlockSpec((tm, d_c), lambda i: (i, 0)),
        ],
        out_shape=[
            jax.ShapeDtypeStruct((n, d_keep), F32),
            jax.ShapeDtypeStruct((n, d_c), F32),
        ],
        scratch_shapes=[slab, slab],
        compiler_params=pltpu.CompilerParams(
            dimension_semantics=("arbitrary",), vmem_limit_bytes=V7X_VMEM_LIMIT),
        name="mix_in_proj",
    )(h, mod8, g_pre, w_in, w_s, gnorm, bias_s)


def _rglru_tile(zm_ref, zp_ref, zn_ref, has_prev, has_next, cw_ref, cb_ref, wax_ref, bax_ref, lam_ref,
                a_s, b_s, carry, hout_ref, *, reverse):
    tm, da = zm_ref.shape
    ext = jnp.concatenate([zp_ref[...] * has_prev, zm_ref[...], zn_ref[...] * has_next], axis=0)
    cw = cw_ref[...]
    xa = cb_ref[...]
    for k in range(CONV_A):
        xa = xa + ext[k * SUBLANES:k * SUBLANES + tm] * cw[k:k + 1]
    pre = _bdot(xa.astype(BF16), wax_ref[...]) + bax_ref[...]
    r = _sigmoid(pre[:, :da])
    gate_i = _sigmoid(pre[:, da:])
    lam = lam_ref[...]
    softplus_neg_lam = jnp.maximum(-lam, 0.0) + jnp.log1p(jnp.exp(-jnp.abs(lam)))
    log_a = (-LRU_C * r) * softplus_neg_lam
    a = jnp.exp(log_a)
    one_minus_a2 = -jnp.tanh(log_a) * (a * a + 1.0)
    a_s[...] = a
    b_s[...] = jnp.sqrt(one_minus_a2) * (gate_i * xa)
    h = carry[...]
    steps = tm // SUBLANES
    order = range(steps - 1, -1, -1) if reverse else range(steps)
    for t in order:
        sl = pl.ds(t * SUBLANES, SUBLANES)
        h = a_s[sl, :] * h + b_s[sl, :]
        hout_ref[sl, :] = h
    carry[...] = h


def _scan_kernel(zm_ref, zp_ref, zn_ref, cw_ref, cb_ref, wax_ref, bax_ref, lam_ref, h0_ref,
                 hout_ref, state_ref, a_s, b_s, carry, *, reverse):
    i = pl.program_id(0)
    nt = pl.num_programs(0)
    tile = nt - 1 - i if reverse else i

    @pl.when(i == 0)
    def _():
        carry[...] = h0_ref[...]

    has_prev = (tile > 0).astype(F32)
    has_next = (tile < nt - 1).astype(F32)
    _rglru_tile(zm_ref, zp_ref, zn_ref, has_prev, has_next, cw_ref, cb_ref, wax_ref, bax_ref, lam_ref,
                a_s, b_s, carry, hout_ref, reverse=reverse)
    state_ref[...] = carry[...]


def _halo_specs(tm, width, col_block, n_rows, before, after, tile_of):
    pb = tm // before
    nb = tm // after
    last_nb = n_rows // after - 1
    prev = pl.BlockSpec((before, width), lambda i: (jnp.maximum(tile_of(i) * pb - 1, 0), col_block))
    nxt = pl.BlockSpec((after, width), lambda i: (jnp.minimum((tile_of(i) + 1) * nb, last_nb), col_block))
    return prev, nxt


def _scan_call(z, cw, cb, wax, bax, lam, h0, *, reverse, tm):
    n = z.shape[0]
    da = cw.shape[1]
    nt = n // tm
    tile_of = (lambda i: nt - 1 - i) if reverse else (lambda i: i)
    prev, nxt = _halo_specs(tm, da, 0, n, (CONV_A // 2) * SUBLANES, SUBLANES, tile_of)
    return pl.pallas_call(
        functools.partial(_scan_kernel, reverse=reverse),
        grid=(nt,),
        in_specs=[
            pl.BlockSpec((tm, da), lambda i: (tile_of(i), 0)), prev, nxt,
            _const_spec(cw.shape), _const_spec(cb.shape), _const_spec(wax.shape),
            _const_spec(bax.shape), _const_spec(lam.shape), _const_spec(h0.shape),
        ],
        out_specs=[
            pl.BlockSpec((tm, da), lambda i: (tile_of(i), 0)),
            pl.BlockSpec((SUBLANES, da), lambda i: (0, 0)),
        ],
        out_shape=[
            jax.ShapeDtypeStruct((n, da), F32),
            jax.ShapeDtypeStruct((SUBLANES, da), F32),
        ],
        scratch_shapes=[
            pltpu.VMEM((tm, da), F32), pltpu.VMEM((tm, da), F32), pltpu.VMEM((SUBLANES, da), F32),
        ],
        compiler_params=pltpu.CompilerParams(
            dimension_semantics=("arbitrary",), vmem_limit_bytes=32 * 1024 * 1024),
        name="rglru_scan_bwd" if reverse else "rglru_scan_fwd",
    )(z, z, z, cw, cb, wax, bax, lam, h0)


def _pool_means(ext, zmain_rows, t0, seq_len):
    halo = POOL_HALF[-1]
    e_rows, width = ext.shape
    e = e_rows // SUBLANES
    tm = zmain_rows

    def rows(arr, start, count):
        return arr[start * SUBLANES:(start + count) * SUBLANES]

    s2 = rows(ext, 0, e - 1) + rows(ext, 1, e - 1)
    s4 = rows(s2, 0, e - 3) + rows(s2, 2, e - 3)
    s8 = rows(s4, 0, e - 7) + rows(s4, 4, e - 7)
    s16 = rows(s8, 0, e - 15) + rows(s8, 8, e - 15)
    t = tm // SUBLANES
    sums = (rows(s2, halo - 1, t), rows(s4, halo - 2, t), rows(s8, halo - 4, t), rows(s16, 0, t))
    group = lax.broadcasted_iota(jnp.int32, (1, width), 1) // (width // len(POOL_HALF))
    win = sums[-1]
    half = jnp.full((1, width), POOL_HALF[-1], jnp.int32)
    for g in range(len(POOL_HALF) - 2, -1, -1):
        win = jnp.where(group == g, sums[g], win)
        half = jnp.where(group == g, POOL_HALF[g], half)
    pos = t0 + lax.broadcasted_iota(jnp.int32, (tm, width), 0) // SUBLANES
    lo = jnp.maximum(pos - half, 0)
    hi = jnp.minimum(pos + half, seq_len)
    return win / (hi - lo).astype(F32)


def _mix_kernel(za_ref, zap_ref, zan_ref, zg_ref, zb_ref, zbp_ref, zbn_ref, yc_ref, hb_ref, h_ref,
                mod_ref, gpost_ref, cw_ref, cb_ref, wax_ref, bax_ref, lam_ref, h0_ref,
                pw_ref, pb_ref, ps_ref, wout_ref,
                out_ref, state_ref, a_s, b_s, carry, hf_s, *, seq_len):
    i = pl.program_id(0)
    nt = pl.num_programs(0)
    tm, d = h_ref.shape

    @pl.when(i == 0)
    def _():
        carry[...] = h0_ref[...]

    has_prev = (i > 0).astype(F32)
    has_next = (i < nt - 1).astype(F32)
    _rglru_tile(za_ref, zap_ref, zan_ref, has_prev, has_next, cw_ref, cb_ref, wax_ref, bax_ref, lam_ref,
                a_s, b_s, carry, hf_s, reverse=False)
    state_ref[...] = carry[...]

    ya = _gelu(zg_ref[...]) * (hf_s[...] + hb_ref[...])

    zb = zb_ref[...]
    ext = jnp.concatenate([zbp_ref[...] * has_prev, zb, zbn_ref[...] * has_next], axis=0)
    t0 = i * (tm // SUBLANES)
    p = _pool_means(ext, tm, t0, seq_len) - zb
    yb = (_bdot(p.astype(BF16), pw_ref[...]) + pb_ref[...]) * ps_ref[...]

    cat = jnp.concatenate([ya, yb, yc_ref[...]], axis=1).astype(BF16)
    mix = _bdot(cat, wout_ref[...])
    out_ref[...] = h_ref[...] + _mul_b(_rms(mix, gpost_ref[...]), _mod_slice(mod_ref, 2, d))


def _mix_call(z, yc, hb, h, mod8, g_post, cw, cb, wax, bax, lam, h0, pool_w, pool_b, pool_s, w_out,
              *, seq_len, tm):
    n, d = h.shape
    da = cw.shape[1]
    db = pool_w.shape[0]
    nt = n // tm
    ident = lambda i: i
    za_prev, za_next = _halo_specs(tm, da, 0, n, (CONV_A // 2) * SUBLANES, SUBLANES, ident)
    pool_halo = POOL_HALF[-1] * SUBLANES
    zb_col = (2 * da) // db
    zb_prev, zb_next = _halo_specs(tm, db, zb_col, n, pool_halo, pool_halo, ident)
    consts = (mod8, g_post, cw, cb, wax, bax, lam, h0, pool_w, pool_b, pool_s, w_out)
    return pl.pallas_call(
        functools.partial(_mix_kernel, seq_len=seq_len),
        grid=(nt,),
        in_specs=[
            pl.BlockSpec((tm, da), lambda i: (i, 0)), za_prev, za_next,
            pl.BlockSpec((tm, da), lambda i: (i, 1)),
            pl.BlockSpec((tm, db), lambda i: (i, zb_col)), zb_prev, zb_next,
            pl.BlockSpec((tm, yc.shape[1]), lambda i: (i, 0)),
            pl.BlockSpec((tm, da), lambda i: (i, 0)),
            pl.BlockSpec((tm, d), lambda i: (i, 0)),
        ] + [_const_spec(a.shape) for a in consts],
        out_specs=[
            pl.BlockSpec((tm, d), lambda i: (i, 0)),
            pl.BlockSpec((SUBLANES, da), lambda i: (0, 0)),
        ],
        out_shape=[
            jax.ShapeDtypeStruct((n, d), F32),
            jax.ShapeDtypeStruct((SUBLANES, da), F32),
        ],
        scratch_shapes=[
            pltpu.VMEM((tm, da), F32), pltpu.VMEM((tm, da), F32), pltpu.VMEM((SUBLANES, da), F32),
            pltpu.VMEM((tm, da), F32),
        ],
        compiler_params=pltpu.CompilerParams(
            dimension_semantics=("arbitrary",), vmem_limit_bytes=V7X_VMEM_LIMIT),
        name="token_mix",
    )(z, z, z, z, z, z, z, yc, hb, h, *consts)


def _tile_rows(x8, rows):
    return jnp.broadcast_to(x8[None], (rows // SUBLANES,) + x8.shape).reshape(rows, x8.shape[1])


def _ffn_grid_kernel(hrow_ref, mod_ref, gpre_ref, gpost_ref, wup_ref, cw_ref, cb_ref, wdown_ref,
                     out_ref, ring, hm_s, hres_s, zbuf0, zbuf1, ybuf0, ybuf1, gbuf, *slabs,
                     n_img_rows, batch_major_out):
    zbuf = (zbuf0, zbuf1)
    ybuf = (ybuf0, ybuf1)
    s = pl.program_id(0)
    rm, d = hrow_ref.shape
    n_slabs = d // LANES

    if batch_major_out:
        acc3, = slabs

        def acc_zero():
            acc3[...] = jnp.zeros_like(acc3)

        def acc_add(r):
            for j in range(n_slabs):
                acc3[j] += r[:, j * LANES:(j + 1) * LANES]

        def acc_get():
            return jnp.concatenate([acc3[j] for j in range(n_slabs)], axis=1)

        def emit(res):
            for j in range(n_slabs):
                acc3[j] = res[:, j * LANES:(j + 1) * LANES]
            for j in range(n_slabs):
                for b in range(SUBLANES):
                    out_ref[b, :, j * LANES:(j + 1) * LANES] = acc3[j, pl.ds(b, rm // SUBLANES, stride=SUBLANES), :]
    else:
        def acc_zero():
            out_ref[...] = jnp.zeros_like(out_ref)

        def acc_add(r):
            out_ref[...] += r

        def acc_get():
            return out_ref[...]

        def emit(res):
            out_ref[...] = res
    nch = wdown_ref.shape[0]
    nc = wdown_ref.shape[1]
    rb = FFN_ROW_BLOCK
    nb = rb // SUBLANES
    pad = SUBLANES

    @pl.when(s == 0)
    def _():
        for k in range(2 * nch):
            ring[0, k] = jnp.zeros((rm, nc), F32)
            ring[1, k] = _tile_rows(cb_ref[k], rm)
        hres_s[...] = hrow_ref[...]
        for zb in zbuf:
            zb[...] = jnp.zeros_like(zb)

    hm_s[...] = _modulate(hrow_ref[...], gpre_ref[...], _mod_slice(mod_ref, 3, d),
                          _mod_slice(mod_ref, 4, d)).astype(BF16)
    acc_zero()
    has_row = (s < n_img_rows).astype(F32)

    def up_proj(c, half):
        zbuf[c % 2][half, pad:pad + rm] = _bdot(hm_s[...], wup_ref[half * nch + c])

    grp = FFN_DOWN_GROUP

    def down_proj_group(p):
        n_in = min(grp, nch - grp * p)
        w_grp = wdown_ref[grp * p:grp * p + n_in].reshape(n_in * nc, d)
        acc_add(_bdot(ybuf[p % 2][:, 0:n_in * nc], w_grp))

    def conv_half(c, half):
        slot = c % 2
        k = half * nch + c
        y_dst = ybuf[(c // grp) % 2]
        y_col0 = (c % grp) * nc
        for col in range(nc // LANES):
            ls = slice(col * LANES, (col + 1) * LANES)
            ys = slice(y_col0 + col * LANES, y_col0 + (col + 1) * LANES)
            w = [cw_ref[k, j * SUBLANES:(j + 1) * SUBLANES, ls][None] for j in range(9)]
            w_row = [w[6 + j] * has_row for j in range(3)]
            bias = cb_ref[k, :, ls][None]

            def taps(zl, z, zr, w3):
                return zl * w3[0] + z * w3[1] + zr * w3[2]

            for r0 in range(0, rm, rb):
                rs = slice(r0, r0 + rb)
                win = zbuf[slot][half, r0:r0 + rb + 2 * pad, ls].reshape(nb + 2, SUBLANES, LANES)
                zl, z, zr = win[0:nb], win[1:nb + 1], win[2:nb + 2]
                shape3 = (nb, SUBLANES, LANES)
                done = ring[0, k, rs, ls].reshape(shape3) + taps(zl, z, zr, w_row)
                ring[0, k, rs, ls] = (ring[1, k, rs, ls].reshape(shape3) + taps(zl, z, zr, w[3:6])).reshape(rb, LANES)
                ring[1, k, rs, ls] = (bias + taps(zl, z, zr, w[0:3])).reshape(rb, LANES)
                done = done.reshape(rb, LANES)
                if half == 0:
                    gbuf[rs, ls] = _gelu(done)
                else:
                    y_dst[rs, ys] = (gbuf[rs, ls] * done).astype(BF16)

    up_proj(0, 0)
    up_proj(0, 1)
    for c in range(nch):
        if c + 1 < nch:
            up_proj(c + 1, 0)
        if c > 0 and c % grp == 0:
            down_proj_group(c // grp - 1)
        conv_half(c, 0)
        if c + 1 < nch:
            up_proj(c + 1, 1)
        conv_half(c, 1)
    down_proj_group((nch - 1) // grp)
    emit(hres_s[...] + _mul_b(_rms(acc_get(), gpost_ref[...]), _mod_slice(mod_ref, 5, d)))
    hres_s[...] = hrow_ref[...]


def _ffn_grid_call(h, mod8, g_pre, g_post, w_up, cw, cb, w_down, *, grid_w, batch_major_out):
    n, d = h.shape
    rm = grid_w * SUBLANES
    n_img_rows = n // rm
    nch, nc, _ = w_down.shape
    consts = (mod8, g_pre, g_post, w_up, cw, cb, w_down)
    if batch_major_out:
        out_spec = pl.BlockSpec((SUBLANES, grid_w, d), lambda s: (0, jnp.maximum(s - 1, 0), 0))
        out_shape = jax.ShapeDtypeStruct((SUBLANES, n // SUBLANES, d), F32)
        slabs = [pltpu.VMEM((d // LANES, rm, LANES), F32)]
    else:
        out_spec = pl.BlockSpec((rm, d), lambda s: (jnp.maximum(s - 1, 0), 0))
        out_shape = jax.ShapeDtypeStruct((n, d), F32)
        slabs = []
    return pl.pallas_call(
        functools.partial(_ffn_grid_kernel, n_img_rows=n_img_rows, batch_major_out=batch_major_out),
        grid=(n_img_rows + 1,),
        in_specs=[
            pl.BlockSpec((rm, d), lambda s: (jnp.minimum(s, n_img_rows - 1), 0)),
        ] + [_const_spec(a.shape) for a in consts],
        out_specs=out_spec,
        out_shape=out_shape,
        scratch_shapes=[
            pltpu.VMEM((2, 2 * nch, rm, nc), F32),
            pltpu.VMEM((rm, d), BF16),
            pltpu.VMEM((rm, d), F32),
            pltpu.VMEM((2, rm + 2 * SUBLANES, nc), F32),
            pltpu.VMEM((2, rm + 2 * SUBLANES, nc), F32),
            pltpu.VMEM((rm, FFN_DOWN_GROUP * nc), BF16),
            pltpu.VMEM((rm, FFN_DOWN_GROUP * nc), BF16),
            pltpu.VMEM((rm, nc), F32),
        ] + slabs,
        compiler_params=pltpu.CompilerParams(
            dimension_semantics=("arbitrary",), vmem_limit_bytes=V7X_VMEM_LIMIT_MAX),
        name="conv_ffn_grid_out" if batch_major_out else "conv_ffn_grid",
    )(h, *consts)


def _ffn_seq_kernel(h_ref, hp_ref, hn_ref, mod_ref, gpre_ref, gpost_ref, wup_ref, cw_ref, cb_ref, wdown_ref,
                    out_ref, hm_s, acc_s):
    i = pl.program_id(0)
    nt = pl.num_programs(0)
    tm, d = h_ref.shape
    nch = wdown_ref.shape[0]
    shift8 = _mod_slice(mod_ref, 3, d)
    scale8 = _mod_slice(mod_ref, 4, d)
    gpre = gpre_ref[...]
    has_prev = (i > 0).astype(F32)
    has_next = (i < nt - 1).astype(F32)
    hm_s[0:SUBLANES] = (_modulate(hp_ref[...], gpre, shift8, scale8) * has_prev).astype(BF16)
    hm_s[SUBLANES:SUBLANES + tm] = _modulate(h_ref[...], gpre, shift8, scale8).astype(BF16)
    hm_s[SUBLANES + tm:2 * SUBLANES + tm] = (_modulate(hn_ref[...], gpre, shift8, scale8) * has_next).astype(BF16)
    acc_s[...] = jnp.zeros_like(acc_s)

    def chunk(c, carry_unused):
        done = []
        for half in range(2):
            k = half * nch + c
            z = _bdot(hm_s[...], wup_ref[k])
            w = [cw_ref[k, j * SUBLANES:j * SUBLANES + 1, :] for j in (3, 4, 5)]
            done.append(cb_ref[k, 0:1, :] + z[0:tm] * w[0] + z[SUBLANES:SUBLANES + tm] * w[1]
                        + z[2 * SUBLANES:2 * SUBLANES + tm] * w[2])
        y = (_gelu(done[0]) * done[1]).astype(BF16)
        acc_s[...] += _bdot(y, wdown_ref[c])
        return carry_unused

    lax.fori_loop(0, nch, chunk, 0)
    out_ref[...] = h_ref[...] + _mul_b(_rms(acc_s[...], gpost_ref[...]), _mod_slice(mod_ref, 5, d))


def _ffn_seq_call(h, mod8, g_pre, g_post, w_up, cw, cb, w_down, *, tm):
    n, d = h.shape
    nt = n // tm
    prev, nxt = _halo_specs(tm, d, 0, n, SUBLANES, SUBLANES, lambda i: i)
    consts = (mod8, g_pre, g_post, w_up, cw, cb, w_down)
    return pl.pallas_call(
        _ffn_seq_kernel,
        grid=(nt,),
        in_specs=[pl.BlockSpec((tm, d), lambda i: (i, 0)), prev, nxt] + [_const_spec(a.shape) for a in consts],
        out_specs=pl.BlockSpec((tm, d), lambda i: (i, 0)),
        out_shape=jax.ShapeDtypeStruct((n, d), F32),
        scratch_shapes=[
            pltpu.VMEM((tm + 2 * SUBLANES, d), BF16),
            pltpu.VMEM((tm, d), F32),
        ],
        compiler_params=pltpu.CompilerParams(
            dimension_semantics=("arbitrary",), vmem_limit_bytes=V7X_VMEM_LIMIT),
        name="conv_ffn_seq",
    )(h, h, h, *consts)


def _to_rows(x):
    b, l, d = x.shape
    return jnp.transpose(x, (1, 0, 2)).reshape(l * b, d)


def _block_diag(w):
    hh, ii, jj = w.shape
    tiled = jnp.concatenate([w.reshape(hh * ii, jj)] * hh, axis=1)
    r = lax.broadcasted_iota(jnp.int32, tiled.shape, 0) // ii
    c = lax.broadcasted_iota(jnp.int32, tiled.shape, 1) // jj
    return jnp.where(r == c, tiled, 0.0)


def kernel(x, c, ctx, c_ctx, w_mod, b_mod, g_pre_mix, g_post_mix, g_pre_ffn, g_post_ffn, w_in, conv_a_w, conv_a_b, lru_w_a, lru_b_a, lru_w_x, lru_b_x, lru_lam, pool_w, pool_b, pool_scale, gmlp_norm, gmlp_w_s, gmlp_b_s, w_out, ffn_w_up, ffn_conv_w, ffn_conv_b, ffn_w_down):
    batch, seq_len, d = x.shape
    ctx_len = ctx.shape[1]
    depth = w_mod.shape[0]
    assert batch == SUBLANES, "the (t, b) row layout puts the batch on the 8 sublanes"
    d_a = conv_a_w.shape[2]
    d_b = pool_b.shape[1]
    d_c = gmlp_norm.shape[1]
    d_ff = ffn_w_down.shape[1]
    assert gmlp_w_s.shape[2] == CHUNK and seq_len % CHUNK == 0 and ctx_len % CHUNK == 0
    assert seq_len % GRID_W == 0 and d_ff % FFN_NC == 0
    nch = d_ff // FFN_NC
    d_keep = 2 * d_a + d_b

    h_lat = _to_rows(x)
    h_ctx = _to_rows(ctx)

    pad = jnp.zeros((2 * SUBLANES - batch - 1, d), F32)
    c_all = jnp.concatenate([c, c_ctx[None], pad], axis=0)
    mod = _mod_call(c_all, w_mod, b_mod)
    zero_state = jnp.zeros((SUBLANES, d_a), F32)

    for l in range(depth):
        last = l == depth - 1
        mod_lat = mod[l, :batch]
        mod_ctx = jnp.broadcast_to(mod[l, batch:batch + 1], (SUBLANES, N_MOD * d))
        row = lambda v: v.reshape(1, -1)

        w_in_l = w_in[l].astype(BF16)
        w_s_l = gmlp_w_s[l].reshape(N_GROUPS_C * CHUNK, CHUNK).astype(BF16)
        bias_s = jnp.repeat(gmlp_b_s[l].T, d_c // N_GROUPS_C, axis=1)
        wax = [jnp.concatenate([_block_diag(lru_w_a[l, k]), _block_diag(lru_w_x[l, k])], axis=1).astype(BF16)
               for k in range(2)]
        bax = [jnp.concatenate([lru_b_a[l, k], lru_b_x[l, k]]).reshape(1, -1) for k in range(2)]
        lam = [row(lru_lam[l, k]) for k in range(2)]
        cw_a, cb_a = conv_a_w[l], row(conv_a_b[l])
        pool_w_l = _block_diag(pool_w[l]).astype(BF16)
        w_out_l = w_out[l].astype(BF16)
        w_up_l = jnp.transpose(ffn_w_up[l].reshape(d, 2 * nch, FFN_NC), (1, 0, 2)).astype(BF16)
        cw_f = jnp.repeat(jnp.transpose(ffn_conv_w[l].reshape(9, 2 * nch, FFN_NC), (1, 0, 2)), SUBLANES, axis=1)
        cb_f = jnp.broadcast_to(ffn_conv_b[l].reshape(2 * nch, 1, FFN_NC), (2 * nch, SUBLANES, FFN_NC))
        w_down_l = ffn_w_down[l].reshape(nch, FFN_NC, d).astype(BF16)

        def mix(h, mod8, h0_f, h0_b, length):
            z, yc = _in_call(h, mod8, row(g_pre_mix[l]), w_in_l, w_s_l, row(gmlp_norm[l]), bias_s,
                             d_keep=d_keep, d_c=d_c)
            hb, state_b = _scan_call(z, cw_a, cb_a, wax[1], bax[1], lam[1], h0_b, reverse=True, tm=SEQ_TILE)
            h_new, state_f = _mix_call(z, yc, hb, h, mod8, row(g_post_mix[l]), cw_a, cb_a, wax[0], bax[0],
                                       lam[0], h0_f, pool_w_l, row(pool_b[l]), row(pool_scale[l]), w_out_l,
                                       seq_len=length, tm=SEQ_TILE)
            return h_new, state_f, state_b

        h_ctx_mixed, ctx_f, ctx_b = mix(h_ctx, mod_ctx, zero_state, zero_state, ctx_len)
        h_lat, _, _ = mix(h_lat, mod_lat, ctx_f, ctx_b, seq_len)

        ffn_w = (row(g_pre_ffn[l]), row(g_post_ffn[l]), w_up_l, cw_f, cb_f, w_down_l)
        h_lat = _ffn_grid_call(h_lat, mod_lat, *ffn_w, grid_w=GRID_W, batch_major_out=last)
        if not last:
            h_ctx = _ffn_seq_call(h_ctx_mixed, mod_ctx, *ffn_w, tm=GRID_W * SUBLANES)

    return h_lat
```

```python
import functools
import math

import jax
import jax.numpy as jnp
from jax import lax
from jax.experimental import pallas as pl
from jax.experimental.pallas import tpu as pltpu

GRID_W = 64
CONV_A = 4
LRU_C = 8.0
N_HEADS_A = 8
POOL_HALF = (1, 2, 4, 8)
N_GROUPS_C = 4
CHUNK = 128
N_MOD = 6
EPS = 1e-6
GELU_K = math.sqrt(2.0 / math.pi)
GELU_C = 0.044715

SUBLANES = 8
LANES = 128
V7X_VMEM_LIMIT = 56 * 1024 * 1024

V7X_VMEM_LIMIT_MAX = 62 * 1024 * 1024

SEQ_TILE = 1024
FFN_NC = 256
FFN_DOWN_GROUP = 2
FFN_ROW_BLOCK = 32

F32 = jnp.float32
BF16 = jnp.bfloat16


def _sigmoid(x):
    return 0.5 * (jnp.tanh(0.5 * x) + 1.0)


def _gelu(x):
    th = jnp.tanh(x * (GELU_K + (GELU_K * GELU_C) * (x * x)))
    hx = 0.5 * x
    return hx + hx * th


def _rms(x, g):
    ms = jnp.mean(x * x, axis=-1, keepdims=True)
    return x * lax.rsqrt(ms + EPS) * g


def _bdot(a, b):
    return jnp.dot(a, b, preferred_element_type=F32)


def _mul_b(x, v8):
    r, c = x.shape
    return (x.reshape(r // SUBLANES, SUBLANES, c) * v8[None]).reshape(r, c)


def _add_b(x, v8):
    r, c = x.shape
    return (x.reshape(r // SUBLANES, SUBLANES, c) + v8[None]).reshape(r, c)


def _modulate(x, g, shift8, scale8):
    return _add_b(_mul_b(_rms(x, g), 1.0 + scale8), shift8)


def _mod_slice(mod_ref, k, d):
    return mod_ref[:, k * d:(k + 1) * d]


def _const_spec(shape):
    nd = len(shape)
    return pl.BlockSpec(shape, lambda *_: (0,) * nd, pipeline_mode=pl.Buffered(1))


def _mod_kernel(c_ref, w_ref, b_ref, o_ref):
    c = c_ref[...]
    s = c * _sigmoid(c)
    o_ref[0] = jnp.dot(s, w_ref[0], preferred_element_type=F32,
                       precision=lax.Precision.HIGHEST) + b_ref[0]


def _mod_call(c_all, w_mod, b_mod):
    depth, d, nm = w_mod.shape
    rows = c_all.shape[0]
    nt = 1536
    return pl.pallas_call(
        _mod_kernel,
        grid=(depth, nm // nt),
        in_specs=[
            pl.BlockSpec((rows, d), lambda l, j: (0, 0)),
            pl.BlockSpec((1, d, nt), lambda l, j: (l, 0, j)),
            pl.BlockSpec((1, 1, nt), lambda l, j: (l, 0, j)),
        ],
        out_specs=pl.BlockSpec((1, rows, nt), lambda l, j: (l, 0, j)),
        out_shape=jax.ShapeDtypeStruct((depth, rows, nm), F32),
        compiler_params=pltpu.CompilerParams(
            dimension_semantics=("arbitrary", "arbitrary"),
            vmem_limit_bytes=40 * 1024 * 1024),
        name="mod_vectors",
    )(c_all, w_mod, b_mod.reshape(depth, 1, nm))


def _in_kernel(h_ref, mod_ref, g_ref, w_ref, ws_ref, gn_ref, bs_ref, z_ref, yc_ref, vn_s, sg_s, *, d_keep, d_c):
    d = h_ref.shape[1]
    tm = h_ref.shape[0]
    npos = tm // SUBLANES
    hm = _modulate(h_ref[...], g_ref[...], _mod_slice(mod_ref, 0, d), _mod_slice(mod_ref, 1, d))
    z = _bdot(hm.astype(BF16), w_ref[...])
    z_ref[...] = z[:, :d_keep]
    zc = _gelu(z[:, d_keep:])
    u = zc[:, :d_c]
    v = zc[:, d_c:]
    mu = jnp.mean(v, axis=-1, keepdims=True)
    vc = v - mu
    var = jnp.mean(vc * vc, axis=-1, keepdims=True)
    vn = vc * lax.rsqrt(var + EPS) * gn_ref[...]
    n_slabs = d_c // LANES
    for j in range(n_slabs):
        vn_s[j] = vn[:, j * LANES:(j + 1) * LANES]
    group = lax.broadcasted_iota(jnp.int32, (1, d_c), 1) // (d_c // N_GROUPS_C)
    rows_of = [pl.ds(b, npos, stride=SUBLANES) for b in range(SUBLANES)]
    v_all = jnp.concatenate([vn_s[j, rows_of[b], :] for b in range(SUBLANES) for j in range(n_slabs)],
                            axis=1).astype(BF16)
    mixed = _bdot(ws_ref[...], v_all)
    for b in range(SUBLANES):
        sb = bs_ref[...]
        for g in range(N_GROUPS_C):
            sb = sb + jnp.where(group == g, mixed[g * npos:(g + 1) * npos, b * d_c:(b + 1) * d_c], 0.0)
        for j in range(n_slabs):
            sg_s[j, rows_of[b], :] = sb[:, j * LANES:(j + 1) * LANES]
    s = jnp.concatenate([sg_s[j] for j in range(n_slabs)], axis=1)
    yc_ref[...] = u * s


def _in_call(h, mod8, g_pre, w_in, w_s, gnorm, bias_s, *, d_keep, d_c):
    n, d = h.shape
    tm = CHUNK * SUBLANES
    slab = pltpu.VMEM((d_c // LANES, tm, LANES), F32)
    return pl.pallas_call(
        functools.partial(_in_kernel, d_keep=d_keep, d_c=d_c),
        grid=(n // tm,),
        in_specs=[
            pl.BlockSpec((tm, d), lambda i: (i, 0)),
            _const_spec(mod8.shape),
            _const_spec(g_pre.shape),
            _const_spec(w_in.shape),
            _const_spec(w_s.shape),
            _const_spec(gnorm.shape),
            _const_spec(bias_s.shape),
        ],
        out_specs=[
            pl.BlockSpec((tm, d_keep), lambda i: (i, 0)),
            pl.BlockSpec((tm, d_c), lambda i: (i, 0)),
        ],
        out_shape=[
            jax.ShapeDtypeStruct((n, d_keep), F32),
            jax.ShapeDtypeStruct((n, d_c), F32),
        ],
        scratch_shapes=[slab, slab],
        compiler_params=pltpu.CompilerParams(
            dimension_semantics=("arbitrary",), vmem_limit_bytes=V7X_VMEM_LIMIT),
        name="mix_in_proj",
    )(h, mod8, g_pre, w_in, w_s, gnorm, bias_s)


def _rglru_tile(zm_ref, zp_ref, zn_ref, has_prev, has_next, cw_ref, cb_ref, wax_ref, bax_ref, lam_ref,
                a_s, b_s, carry, hout_ref, *, reverse):
    tm, da = zm_ref.shape
    ext = jnp.concatenate([zp_ref[...] * has_prev, zm_ref[...], zn_ref[...] * has_next], axis=0)
    cw = cw_ref[...]
    xa = cb_ref[...]
    for k in range(CONV_A):
        xa = xa + ext[k * SUBLANES:k * SUBLANES + tm] * cw[k:k + 1]
    pre = _bdot(xa.astype(BF16), wax_ref[...]) + bax_ref[...]
    r = _sigmoid(pre[:, :da])
    gate_i = _sigmoid(pre[:, da:])
    lam = lam_ref[...]
    softplus_neg_lam = jnp.maximum(-lam, 0.0) + jnp.log1p(jnp.exp(-jnp.abs(lam)))
    log_a = (-LRU_C * r) * softplus_neg_lam
    a = jnp.exp(log_a)
    one_minus_a2 = -jnp.tanh(log_a) * (a * a + 1.0)
    a_s[...] = a
    b_s[...] = jnp.sqrt(one_minus_a2) * (gate_i * xa)
    h = carry[...]
    steps = tm // SUBLANES
    order = range(steps - 1, -1, -1) if reverse else range(steps)
    for t in order:
        sl = pl.ds(t * SUBLANES, SUBLANES)
        h = a_s[sl, :] * h + b_s[sl, :]
        hout_ref[sl, :] = h
    carry[...] = h


def _scan_kernel(zm_ref, zp_ref, zn_ref, cw_ref, cb_ref, wax_ref, bax_ref, lam_ref, h0_ref,
                 hout_ref, state_ref, a_s, b_s, carry, *, reverse):
    i = pl.program_id(0)
    nt = pl.num_programs(0)
    tile = nt - 1 - i if reverse else i

    @pl.when(i == 0)
    def _():
        carry[...] = h0_ref[...]

    has_prev = (tile > 0).astype(F32)
    has_next = (tile < nt - 1).astype(F32)
    _rglru_tile(zm_ref, zp_ref, zn_ref, has_prev, has_next, cw_ref, cb_ref, wax_ref, bax_ref, lam_ref,
                a_s, b_s, carry, hout_ref, reverse=reverse)
    state_ref[...] = carry[...]


def _halo_specs(tm, width, col_block, n_rows, before, after, tile_of):
    pb = tm // before
    nb = tm // after
    last_nb = n_rows // after - 1
    prev = pl.BlockSpec((before, width), lambda i: (jnp.maximum(tile_of(i) * pb - 1, 0), col_block))
    nxt = pl.BlockSpec((after, width), lambda i: (jnp.minimum((tile_of(i) + 1) * nb, last_nb), col_block))
    return prev, nxt


def _scan_call(z, cw, cb, wax, bax, lam, h0, *, reverse, tm):
    n = z.shape[0]
    da = cw.shape[1]
    nt = n // tm
    tile_of = (lambda i: nt - 1 - i) if reverse else (lambda i: i)
    prev, nxt = _halo_specs(tm, da, 0, n, (CONV_A // 2) * SUBLANES, SUBLANES, tile_of)
    return pl.pallas_call(
        functools.partial(_scan_kernel, reverse=reverse),
        grid=(nt,),
        in_specs=[
            pl.BlockSpec((tm, da), lambda i: (tile_of(i), 0)), prev, nxt,
            _const_spec(cw.shape), _const_spec(cb.shape), _const_spec(wax.shape),
            _const_spec(bax.shape), _const_spec(lam.shape), _const_spec(h0.shape),
        ],
        out_specs=[
            pl.BlockSpec((tm, da), lambda i: (tile_of(i), 0)),
            pl.BlockSpec((SUBLANES, da), lambda i: (0, 0)),
        ],
        out_shape=[
            jax.ShapeDtypeStruct((n, da), F32),
            jax.ShapeDtypeStruct((SUBLANES, da), F32),
        ],
        scratch_shapes=[
            pltpu.VMEM((tm, da), F32), pltpu.VMEM((tm, da), F32), pltpu.VMEM((SUBLANES, da), F32),
        ],
        compiler_params=pltpu.CompilerParams(
            dimension_semantics=("arbitrary",), vmem_limit_bytes=32 * 1024 * 1024),
        name="rglru_scan_bwd" if reverse else "rglru_scan_fwd",
    )(z, z, z, cw, cb, wax, bax, lam, h0)


def _pool_means(ext, zmain_rows, t0, seq_len):
    halo = POOL_HALF[-1]
    e_rows, width = ext.shape
    e = e_rows // SUBLANES
    tm = zmain_rows

    def rows(arr, start, count):
        return arr[start * SUBLANES:(start + count) * SUBLANES]

    s2 = rows(ext, 0, e - 1) + rows(ext, 1, e - 1)
    s4 = rows(s2, 0, e - 3) + rows(s2, 2, e - 3)
    s8 = rows(s4, 0, e - 7) + rows(s4, 4, e - 7)
    s16 = rows(s8, 0, e - 15) + rows(s8, 8, e - 15)
    t = tm // SUBLANES
    sums = (rows(s2, halo - 1, t), rows(s4, halo - 2, t), rows(s8, halo - 4, t), rows(s16, 0, t))
    group = lax.broadcasted_iota(jnp.int32, (1, width), 1) // (width // len(POOL_HALF))
    win = sums[-1]
    half = jnp.full((1, width), POOL_HALF[-1], jnp.int32)
    for g in range(len(POOL_HALF) - 2, -1, -1):
        win = jnp.where(group == g, sums[g], win)
        half = jnp.where(group == g, POOL_HALF[g], half)
    pos = t0 + lax.broadcasted_iota(jnp.int32, (tm, width), 0) // SUBLANES
    lo = jnp.maximum(pos - half, 0)
    hi = jnp.minimum(pos + half, seq_len)
    return win / (hi - lo).astype(F32)


def _mix_kernel(za_ref, zap_ref, zan_ref, zg_ref, zb_ref, zbp_ref, zbn_ref, yc_ref, hb_ref, h_ref,
                mod_ref, gpost_ref, cw_ref, cb_ref, wax_ref, bax_ref, lam_ref, h0_ref,
                pw_ref, pb_ref, ps_ref, wout_ref,
                out_ref, state_ref, a_s, b_s, carry, hf_s, *, seq_len):
    i = pl.program_id(0)
    nt = pl.num_programs(0)
    tm, d = h_ref.shape

    @pl.when(i == 0)
    def _():
        carry[...] = h0_ref[...]

    has_prev = (i > 0).astype(F32)
    has_next = (i < nt - 1).astype(F32)
    _rglru_tile(za_ref, zap_ref, zan_ref, has_prev, has_next, cw_ref, cb_ref, wax_ref, bax_ref, lam_ref,
                a_s, b_s, carry, hf_s, reverse=False)
    state_ref[...] = carry[...]

    ya = _gelu(zg_ref[...]) * (hf_s[...] + hb_ref[...])

    zb = zb_ref[...]
    ext = jnp.concatenate([zbp_ref[...] * has_prev, zb, zbn_ref[...] * has_next], axis=0)
    t0 = i * (tm // SUBLANES)
    p = _pool_means(ext, tm, t0, seq_len) - zb
    yb = (_bdot(p.astype(BF16), pw_ref[...]) + pb_ref[...]) * ps_ref[...]

    cat = jnp.concatenate([ya, yb, yc_ref[...]], axis=1).astype(BF16)
    mix = _bdot(cat, wout_ref[...])
    out_ref[...] = h_ref[...] + _mul_b(_rms(mix, gpost_ref[...]), _mod_slice(mod_ref, 2, d))


def _mix_call(z, yc, hb, h, mod8, g_post, cw, cb, wax, bax, lam, h0, pool_w, pool_b, pool_s, w_out,
              *, seq_len, tm):
    n, d = h.shape
    da = cw.shape[1]
    db = pool_w.shape[0]
    nt = n // tm
    ident = lambda i: i
    za_prev, za_next = _halo_specs(tm, da, 0, n, (CONV_A // 2) * SUBLANES, SUBLANES, ident)
    pool_halo = POOL_HALF[-1] * SUBLANES
    zb_col = (2 * da) // db
    zb_prev, zb_next = _halo_specs(tm, db, zb_col, n, pool_halo, pool_halo, ident)
    consts = (mod8, g_post, cw, cb, wax, bax, lam, h0, pool_w, pool_b, pool_s, w_out)
    return pl.pallas_call(
        functools.partial(_mix_kernel, seq_len=seq_len),
        grid=(nt,),
        in_specs=[
            pl.BlockSpec((tm, da), lambda i: (i, 0)), za_prev, za_next,
            pl.BlockSpec((tm, da), lambda i: (i, 1)),
            pl.BlockSpec((tm, db), lambda i: (i, zb_col)), zb_prev, zb_next,
            pl.BlockSpec((tm, yc.shape[1]), lambda i: (i, 0)),
            pl.BlockSpec((tm, da), lambda i: (i, 0)),
            pl.BlockSpec((tm, d), lambda i: (i, 0)),
        ] + [_const_spec(a.shape) for a in consts],
        out_specs=[
            pl.BlockSpec((tm, d), lambda i: (i, 0)),
            pl.BlockSpec((SUBLANES, da), lambda i: (0, 0)),
        ],
        out_shape=[
            jax.ShapeDtypeStruct((n, d), F32),
            jax.ShapeDtypeStruct((SUBLANES, da), F32),
        ],
        scratch_shapes=[
            pltpu.VMEM((tm, da), F32), pltpu.VMEM((tm, da), F32), pltpu.VMEM((SUBLANES, da), F32),
            pltpu.VMEM((tm, da), F32),
        ],
        compiler_params=pltpu.CompilerParams(
            dimension_semantics=("arbitrary",), vmem_limit_bytes=V7X_VMEM_LIMIT),
        name="token_mix",
    )(z, z, z, z, z, z, z, yc, hb, h, *consts)


def _tile_rows(x8, rows):
    return jnp.broadcast_to(x8[None], (rows // SUBLANES,) + x8.shape).reshape(rows, x8.shape[1])


def _ffn_grid_kernel(hrow_ref, mod_ref, gpre_ref, gpost_ref, wup_ref, cw_ref, cb_ref, wdown_ref,
                     out_ref, ring, hm_s, hres_s, zbuf0, zbuf1, ybuf0, ybuf1, gbuf, *slabs,
                     n_img_rows, batch_major_out):
    zbuf = (zbuf0, zbuf1)
    ybuf = (ybuf0, ybuf1)
    s = pl.program_id(0)
    rm, d = hrow_ref.shape
    n_slabs = d // LANES

    if batch_major_out:
        acc3, = slabs

        def acc_zero():
            acc3[...] = jnp.zeros_like(acc3)

        def acc_add(r):
            for j in range(n_slabs):
                acc3[j] += r[:, j * LANES:(j + 1) * LANES]

        def acc_get():
            return jnp.concatenate([acc3[j] for j in range(n_slabs)], axis=1)

        def emit(res):
            for j in range(n_slabs):
                acc3[j] = res[:, j * LANES:(j + 1) * LANES]
            for j in range(n_slabs):
                for b in range(SUBLANES):
                    out_ref[b, :, j * LANES:(j + 1) * LANES] = acc3[j, pl.ds(b, rm // SUBLANES, stride=SUBLANES), :]
    else:
        def acc_zero():
            out_ref[...] = jnp.zeros_like(out_ref)

        def acc_add(r):
            out_ref[...] += r

        def acc_get():
            return out_ref[...]

        def emit(res):
            out_ref[...] = res
    nch = wdown_ref.shape[0]
    nc = wdown_ref.shape[1]
    rb = FFN_ROW_BLOCK
    nb = rb // SUBLANES
    pad = SUBLANES

    @pl.when(s == 0)
    def _():
        for k in range(2 * nch):
            ring[0, k] = jnp.zeros((rm, nc), F32)
            ring[1, k] = _tile_rows(cb_ref[k], rm)
        hres_s[...] = hrow_ref[...]
        for zb in zbuf:
            zb[...] = jnp.zeros_like(zb)

    hm_s[...] = _modulate(hrow_ref[...], gpre_ref[...], _mod_slice(mod_ref, 3, d),
                          _mod_slice(mod_ref, 4, d)).astype(BF16)
    acc_zero()
    has_row = (s < n_img_rows).astype(F32)

    def up_proj(c, half):
        zbuf[c % 2][half, pad:pad + rm] = _bdot(hm_s[...], wup_ref[half * nch + c])

    grp = FFN_DOWN_GROUP

    def down_proj_group(p):
        n_in = min(grp, nch - grp * p)
        w_grp = wdown_ref[grp * p:grp * p + n_in].reshape(n_in * nc, d)
        acc_add(_bdot(ybuf[p % 2][:, 0:n_in * nc], w_grp))

    def conv_half(c, half):
        slot = c % 2
        k = half * nch + c
        y_dst = ybuf[(c // grp) % 2]
        y_col0 = (c % grp) * nc
        for col in range(nc // LANES):
            ls = slice(col * LANES, (col + 1) * LANES)
            ys = slice(y_col0 + col * LANES, y_col0 + (col + 1) * LANES)
            w = [cw_ref[k, j * SUBLANES:(j + 1) * SUBLANES, ls][None] for j in range(9)]
            w_row = [w[6 + j] * has_row for j in range(3)]
            bias = cb_ref[k, :, ls][None]

            def taps(zl, z, zr, w3):
                return zl * w3[0] + z * w3[1] + zr * w3[2]

            for r0 in range(0, rm, rb):
                rs = slice(r0, r0 + rb)
                win = zbuf[slot][half, r0:r0 + rb + 2 * pad, ls].reshape(nb + 2, SUBLANES, LANES)
                zl, z, zr = win[0:nb], win[1:nb + 1], win[2:nb + 2]
                shape3 = (nb, SUBLANES, LANES)
                done = ring[0, k, rs, ls].reshape(shape3) + taps(zl, z, zr, w_row)
                ring[0, k, rs, ls] = (ring[1, k, rs, ls].reshape(shape3) + taps(zl, z, zr, w[3:6])).reshape(rb, LANES)
                ring[1, k, rs, ls] = (bias + taps(zl, z, zr, w[0:3])).reshape(rb, LANES)
                done = done.reshape(rb, LANES)
                if half == 0:
                    gbuf[rs, ls] = _gelu(done)
                else:
                    y_dst[rs, ys] = (gbuf[rs, ls] * done).astype(BF16)

    up_proj(0, 0)
    up_proj(0, 1)
    for c in range(nch):
        if c + 1 < nch:
            up_proj(c + 1, 0)
        if c > 0 and c % grp == 0:
            down_proj_group(c // grp - 1)
        conv_half(c, 0)
        if c + 1 < nch:
            up_proj(c + 1, 1)
        conv_half(c, 1)
    down_proj_group((nch - 1) // grp)
    emit(hres_s[...] + _mul_b(_rms(acc_get(), gpost_ref[...]), _mod_slice(mod_ref, 5, d)))
    hres_s[...] = hrow_ref[...]


def _ffn_grid_call(h, mod8, g_pre, g_post, w_up, cw, cb, w_down, *, grid_w, batch_major_out):
    n, d = h.shape
    rm = grid_w * SUBLANES
    n_img_rows = n // rm
    nch, nc, _ = w_down.shape
    consts = (mod8, g_pre, g_post, w_up, cw, cb, w_down)
    if batch_major_out:
        out_spec = pl.BlockSpec((SUBLANES, grid_w, d), lambda s: (0, jnp.maximum(s - 1, 0), 0))
        out_shape = jax.ShapeDtypeStruct((SUBLANES, n // SUBLANES, d), F32)
        slabs = [pltpu.VMEM((d // LANES, rm, LANES), F32)]
    else:
        out_spec = pl.BlockSpec((rm, d), lambda s: (jnp.maximum(s - 1, 0), 0))
        out_shape = jax.ShapeDtypeStruct((n, d), F32)
        slabs = []
    return pl.pallas_call(
        functools.partial(_ffn_grid_kernel, n_img_rows=n_img_rows, batch_major_out=batch_major_out),
        grid=(n_img_rows + 1,),
        in_specs=[
            pl.BlockSpec((rm, d), lambda s: (jnp.minimum(s, n_img_rows - 1), 0)),
        ] + [_const_spec(a.shape) for a in consts],
        out_specs=out_spec,
        out_shape=out_shape,
        scratch_shapes=[
            pltpu.VMEM((2, 2 * nch, rm, nc), F32),
            pltpu.VMEM((rm, d), BF16),
            pltpu.VMEM((rm, d), F32),
            pltpu.VMEM((2, rm + 2 * SUBLANES, nc), F32),
            pltpu.VMEM((2, rm + 2 * SUBLANES, nc), F32),
            pltpu.VMEM((rm, FFN_DOWN_GROUP * nc), BF16),
            pltpu.VMEM((rm, FFN_DOWN_GROUP * nc), BF16),
            pltpu.VMEM((rm, nc), F32),
        ] + slabs,
        compiler_params=pltpu.CompilerParams(
            dimension_semantics=("arbitrary",), vmem_limit_bytes=V7X_VMEM_LIMIT_MAX),
        name="conv_ffn_grid_out" if batch_major_out else "conv_ffn_grid",
    )(h, *consts)


def _ffn_seq_kernel(h_ref, hp_ref, hn_ref, mod_ref, gpre_ref, gpost_ref, wup_ref, cw_ref, cb_ref, wdown_ref,
                    out_ref, hm_s, acc_s):
    i = pl.program_id(0)
    nt = pl.num_programs(0)
    tm, d = h_ref.shape
    nch = wdown_ref.shape[0]
    shift8 = _mod_slice(mod_ref, 3, d)
    scale8 = _mod_slice(mod_ref, 4, d)
    gpre = gpre_ref[...]
    has_prev = (i > 0).astype(F32)
    has_next = (i < nt - 1).astype(F32)
    hm_s[0:SUBLANES] = (_modulate(hp_ref[...], gpre, shift8, scale8) * has_prev).astype(BF16)
    hm_s[SUBLANES:SUBLANES + tm] = _modulate(h_ref[...], gpre, shift8, scale8).astype(BF16)
    hm_s[SUBLANES + tm:2 * SUBLANES + tm] = (_modulate(hn_ref[...], gpre, shift8, scale8) * has_next).astype(BF16)
    acc_s[...] = jnp.zeros_like(acc_s)

    def chunk(c, carry_unused):
        done = []
        for half in range(2):
            k = half * nch + c
            z = _bdot(hm_s[...], wup_ref[k])
            w = [cw_ref[k, j * SUBLANES:j * SUBLANES + 1, :] for j in (3, 4, 5)]
            done.append(cb_ref[k, 0:1, :] + z[0:tm] * w[0] + z[SUBLANES:SUBLANES + tm] * w[1]
                        + z[2 * SUBLANES:2 * SUBLANES + tm] * w[2])
        y = (_gelu(done[0]) * done[1]).astype(BF16)
        acc_s[...] += _bdot(y, wdown_ref[c])
        return carry_unused

    lax.fori_loop(0, nch, chunk, 0)
    out_ref[...] = h_ref[...] + _mul_b(_rms(acc_s[...], gpost_ref[...]), _mod_slice(mod_ref, 5, d))


def _ffn_seq_call(h, mod8, g_pre, g_post, w_up, cw, cb, w_down, *, tm):
    n, d = h.shape
    nt = n // tm
    prev, nxt = _halo_specs(tm, d, 0, n, SUBLANES, SUBLANES, lambda i: i)
    consts = (mod8, g_pre, g_post, w_up, cw, cb, w_down)
    return pl.pallas_call(
        _ffn_seq_kernel,
        grid=(nt,),
        in_specs=[pl.BlockSpec((tm, d), lambda i: (i, 0)), prev, nxt] + [_const_spec(a.shape) for a in consts],
        out_specs=pl.BlockSpec((tm, d), lambda i: (i, 0)),
        out_shape=jax.ShapeDtypeStruct((n, d), F32),
        scratch_shapes=[
            pltpu.VMEM((tm + 2 * SUBLANES, d), BF16),
            pltpu.VMEM((tm, d), F32),
        ],
        compiler_params=pltpu.CompilerParams(
            dimension_semantics=("arbitrary",), vmem_limit_bytes=V7X_VMEM_LIMIT),
        name="conv_ffn_seq",
    )(h, h, h, *consts)


def _to_rows(x):
    b, l, d = x.shape
    return jnp.transpose(x, (1, 0, 2)).reshape(l * b, d)


def _block_diag(w):
    hh, ii, jj = w.shape
    tiled = jnp.concatenate([w.reshape(hh * ii, jj)] * hh, axis=1)
    r = lax.broadcasted_iota(jnp.int32, tiled.shape, 0) // ii
    c = lax.broadcasted_iota(jnp.int32, tiled.shape, 1) // jj
    return jnp.where(r == c, tiled, 0.0)


def kernel(x, c, ctx, c_ctx, w_mod, b_mod, g_pre_mix, g_post_mix, g_pre_ffn, g_post_ffn, w_in, conv_a_w, conv_a_b, lru_w_a, lru_b_a, lru_w_x, lru_b_x, lru_lam, pool_w, pool_b, pool_scale, gmlp_norm, gmlp_w_s, gmlp_b_s, w_out, ffn_w_up, ffn_conv_w, ffn_conv_b, ffn_w_down):
    batch, seq_len, d = x.shape
    ctx_len = ctx.shape[1]
    depth = w_mod.shape[0]
    assert batch == SUBLANES, "the (t, b) row layout puts the batch on the 8 sublanes"
    d_a = conv_a_w.shape[2]
    d_b = pool_b.shape[1]
    d_c = gmlp_norm.shape[1]
    d_ff = ffn_w_down.shape[1]
    assert gmlp_w_s.shape[2] == CHUNK and seq_len % CHUNK == 0 and ctx_len % CHUNK == 0
    assert seq_len % GRID_W == 0 and d_ff % FFN_NC == 0
    nch = d_ff // FFN_NC
    d_keep = 2 * d_a + d_b

    h_lat = _to_rows(x)
    h_ctx = _to_rows(ctx)

    pad = jnp.zeros((2 * SUBLANES - batch - 1, d), F32)
    c_all = jnp.concatenate([c, c_ctx[None], pad], axis=0)
    mod = _mod_call(c_all, w_mod, b_mod)
    zero_state = jnp.zeros((SUBLANES, d_a), F32)

    for l in range(depth):
        last = l == depth - 1
        mod_lat = mod[l, :batch]
        mod_ctx = jnp.broadcast_to(mod[l, batch:batch + 1], (SUBLANES, N_MOD * d))
        row = lambda v: v.reshape(1, -1)

        w_in_l = w_in[l].astype(BF16)
        w_s_l = gmlp_w_s[l].reshape(N_GROUPS_C * CHUNK, CHUNK).astype(BF16)
        bias_s = jnp.repeat(gmlp_b_s[l].T, d_c // N_GROUPS_C, axis=1)
        wax = [jnp.concatenate([_block_diag(lru_w_a[l, k]), _block_diag(lru_w_x[l, k])], axis=1).astype(BF16)
               for k in range(2)]
        bax = [jnp.concatenate([lru_b_a[l, k], lru_b_x[l, k]]).reshape(1, -1) for k in range(2)]
        lam = [row(lru_lam[l, k]) for k in range(2)]
        cw_a, cb_a = conv_a_w[l], row(conv_a_b[l])
        pool_w_l = _block_diag(pool_w[l]).astype(BF16)
        w_out_l = w_out[l].astype(BF16)
        w_up_l = jnp.transpose(ffn_w_up[l].reshape(d, 2 * nch, FFN_NC), (1, 0, 2)).astype(BF16)
        cw_f = jnp.repeat(jnp.transpose(ffn_conv_w[l].reshape(9, 2 * nch, FFN_NC), (1, 0, 2)), SUBLANES, axis=1)
        cb_f = jnp.broadcast_to(ffn_conv_b[l].reshape(2 * nch, 1, FFN_NC), (2 * nch, SUBLANES, FFN_NC))
        w_down_l = ffn_w_down[l].reshape(nch, FFN_NC, d).astype(BF16)

        def mix(h, mod8, h0_f, h0_b, length, scans_only=False):
            z, yc = _in_call(h, mod8, row(g_pre_mix[l]), w_in_l, w_s_l, row(gmlp_norm[l]), bias_s,
                             d_keep=d_keep, d_c=d_c)
            hb, state_b = _scan_call(z, cw_a, cb_a, wax[1], bax[1], lam[1], h0_b, reverse=True, tm=SEQ_TILE)
            if scans_only:
                _, state_f = _scan_call(z, cw_a, cb_a, wax[0], bax[0], lam[0], h0_f, reverse=False, tm=SEQ_TILE)
                return None, state_f, state_b
            h_new, state_f = _mix_call(z, yc, hb, h, mod8, row(g_post_mix[l]), cw_a, cb_a, wax[0], bax[0],
                                       lam[0], h0_f, pool_w_l, row(pool_b[l]), row(pool_scale[l]), w_out_l,
                                       seq_len=length, tm=SEQ_TILE)
            return h_new, state_f, state_b

        h_ctx_mixed, ctx_f, ctx_b = mix(h_ctx, mod_ctx, zero_state, zero_state, ctx_len, scans_only=last)
        h_lat, _, _ = mix(h_lat, mod_lat, ctx_f, ctx_b, seq_len)

        ffn_w = (row(g_pre_ffn[l]), row(g_post_ffn[l]), w_up_l, cw_f, cb_f, w_down_l)
        h_lat = _ffn_grid_call(h_lat, mod_lat, *ffn_w, grid_w=GRID_W, batch_major_out=last)
        if not last:
            h_ctx = _ffn_seq_call(h_ctx_mixed, mod_ctx, *ffn_w, tm=GRID_W * SUBLANES)

    return h_lat
```
